```python
import math
import jax
import jax.numpy as jnp
from jax import lax
import numpy as np

D_MODEL = 1024
BATCH = 4
SEQ = 8192
DEPTH = 4

A_HEADS = 8
A_HEAD_DIM = 64
A_WIDTH = A_HEADS * A_HEAD_DIM
A_KV_RANK = 128
IDX_HEADS = 8
IDX_DIM = 64
TOPK_MAX = 256
Q_BLOCK = 128
POOL_WINDOWS = (2, 4, 8, 16)
POOL_WIDTH = D_MODEL - A_WIDTH
POOL_GROUP = POOL_WIDTH // len(POOL_WINDOWS)
EVEN_SPLITS = (A_WIDTH, A_WIDTH + A_KV_RANK, A_WIDTH + A_KV_RANK + IDX_HEADS * IDX_DIM, A_WIDTH + A_KV_RANK + IDX_HEADS * IDX_DIM + IDX_DIM, A_WIDTH + A_KV_RANK + IDX_HEADS * IDX_DIM + IDX_DIM + IDX_HEADS)
EVEN_IN = EVEN_SPLITS[-1] + POOL_WIDTH
C_HEADS = 8
C_HEAD_DIM = 128
C_WIDTH = C_HEADS * C_HEAD_DIM
CONV_WIDTH = 4
CHUNK = 64
ODD_SPLITS = (3 * C_WIDTH, 4 * C_WIDTH, 4 * C_WIDTH + C_HEADS)
ODD_IN = 4 * C_WIDTH + 2 * C_HEADS
D_FF = 4 * D_MODEL
N_EVEN = (DEPTH + 1) // 2
N_ODD = DEPTH // 2
ROPE_THETA = 10000.0
NORM_EPS = 1e-6
NEG_SCORE = -1e30

kernel_name = 'hybrid_dsa_pool_gdn_trunk'


def rmsnorm(x, g):
    xf = x.astype(jnp.float32)
    y = xf * lax.rsqrt(jnp.mean(xf * xf, axis=-1, keepdims=True) + NORM_EPS)
    return (y * g.astype(jnp.float32)).astype(x.dtype)


def l2norm(x):
    return x * lax.rsqrt(jnp.sum(x * x, axis=-1, keepdims=True) + NORM_EPS)


def rope_tables(L, dim, dtype):
    inv = ROPE_THETA ** (-jnp.arange(0, dim, 2, dtype=jnp.float32) / dim)
    ang = jnp.arange(L, dtype=jnp.float32)[:, None] * inv[None, :]
    return jnp.cos(ang).astype(dtype), jnp.sin(ang).astype(dtype)


def apply_rope(x, cos, sin):
    x1, x2 = jnp.split(x, 2, axis=-1)
    return jnp.concatenate([x1 * cos - x2 * sin, x1 * sin + x2 * cos], axis=-1)


def dsa_attention(q, k, v, qi, ki, wi):
    B, L, H, Dh = q.shape
    topk = min(TOPK_MAX, L // 4)
    nb = L // Q_BLOCK
    key_pos = jnp.arange(L)
    gather = jax.vmap(lambda t, i: t[i])

    def blocks(t):
        return jnp.moveaxis(t.reshape(B, nb, Q_BLOCK, *t.shape[2:]), 1, 0)

    def one_block(args):
        q_b, qi_b, wi_b, start = args
        q_pos = start + jnp.arange(Q_BLOCK)
        rel = jax.nn.relu(jnp.einsum('bqhd,bsd->bqhs', qi_b, ki).astype(jnp.float32))
        score = jnp.einsum('bqhs,bqh->bqs', rel, wi_b.astype(jnp.float32)) * (IDX_DIM ** -0.5)
        causal = key_pos[None, :] <= q_pos[:, None]
        score = jnp.where(causal[None], score, NEG_SCORE)
        _, idx = lax.top_k(score, topk)
        k_sel = gather(k, idx)
        v_sel = gather(v, idx)
        valid = idx <= q_pos[None, :, None]
        logits = jnp.einsum('bqhd,bqkd->bhqk', q_b, k_sel).astype(jnp.float32) * (Dh ** -0.5)
        logits = jnp.where(valid[:, None], logits, -jnp.inf)
        prob = jax.nn.softmax(logits, axis=-1).astype(v.dtype)
        return jnp.einsum('bhqk,bqkd->bqhd', prob, v_sel)

    out = lax.map(one_block, (blocks(q), blocks(qi), blocks(wi), jnp.arange(nb) * Q_BLOCK))
    return jnp.moveaxis(out, 0, 1).reshape(B, L, H * Dh)


def multiscale_pool(u, pool_w, pool_scale):
    B, L, _ = u.shape
    uf = u.astype(jnp.float32)
    cs = jnp.pad(jnp.cumsum(uf, axis=1), ((0, 0), (1, 0), (0, 0)))
    t = jnp.arange(L)
    outs = []
    for gi, w in enumerate(POOL_WINDOWS):
        lo, hi = gi * POOL_GROUP, (gi + 1) * POOL_GROUP
        csg = cs[..., lo:hi]
        lower = jnp.pad(csg[:, :L - w + 1], ((0, 0), (w - 1, 0), (0, 0)))
        count = jnp.minimum(t + 1, w).astype(jnp.float32)[:, None]
        outs.append((csg[:, 1:] - lower) / count - uf[..., lo:hi])
    pooled = jnp.stack(outs, axis=2).astype(u.dtype)
    y = jnp.einsum('blgc,gcd->blgd', pooled, pool_w).reshape(B, L, POOL_WIDTH)
    return y * pool_scale


def sparse_pool_mixer(h, w_in, kv_norm, w_uk, w_uv, pool_w, pool_scale, w_out):
    B, L, _ = h.shape
    p = h @ w_in
    q, c_kv, qi, ki, wi, u = jnp.split(p, list(EVEN_SPLITS), axis=-1)
    cos, sin = rope_tables(L, A_HEAD_DIM, h.dtype)
    cos_i, sin_i = rope_tables(L, IDX_DIM, h.dtype)
    q = apply_rope(q.reshape(B, L, A_HEADS, A_HEAD_DIM), cos[:, None], sin[:, None])
    c_kv = rmsnorm(c_kv, kv_norm)
    k = apply_rope(c_kv @ w_uk, cos, sin)
    v = c_kv @ w_uv
    qi = apply_rope(qi.reshape(B, L, IDX_HEADS, IDX_DIM), cos_i[:, None], sin_i[:, None])
    ki = apply_rope(ki, cos_i, sin_i)
    wi = wi * (IDX_HEADS ** -0.5)
    ya = dsa_attention(q, k, v, qi, ki, wi)
    yb = multiscale_pool(u, pool_w, pool_scale)
    return jnp.concatenate([ya, yb], axis=-1) @ w_out


def causal_depthwise_conv(x, w):
    K = w.shape[0]
    return lax.conv_general_dilated(x, w[:, None, :], window_strides=(1,), padding=[(K - 1, 0)], dimension_numbers=('NWC', 'WIO', 'NWC'), feature_group_count=x.shape[-1])


def chunk_gated_delta_rule(q, k, v, beta, g):
    B, L, H, Dk = q.shape
    Dv = v.shape[-1]
    N = L // CHUNK

    def to_chunks(t):
        t = jnp.moveaxis(t, 2, 1)
        return t.reshape(B, H, N, CHUNK, *t.shape[3:])

    q, k, v, beta, g = to_chunks(q), to_chunks(k), to_chunks(v), to_chunks(beta), to_chunks(g)
    g = jnp.cumsum(g, axis=-1)
    idx = jnp.arange(CHUNK)
    tril = idx[:, None] >= idx[None, :]
    strict = idx[:, None] > idx[None, :]
    decay = jnp.exp(jnp.where(tril, g[..., :, None] - g[..., None, :], -jnp.inf))
    kk = jnp.einsum('bhnid,bhnjd->bhnij', k, k)
    lmat = jnp.where(strict, beta[..., :, None] * kk * decay, 0.0)
    eye = jnp.eye(CHUNK, dtype=jnp.float32)
    tmat = lax.linalg.triangular_solve(eye + lmat, jnp.broadcast_to(eye, lmat.shape), left_side=True, lower=True, unit_diagonal=True)
    value = tmat @ (v * beta[..., None])
    k_cumdecay = tmat @ (k * (beta * jnp.exp(g))[..., None])
    attn = jnp.einsum('bhnid,bhnjd->bhnij', q, k) * decay
    q_dec = q * jnp.exp(g)[..., None]
    k_dec = k * jnp.exp(g[..., -1:] - g)[..., None]
    g_last = jnp.exp(g[..., -1])

    def step(state, xs):
        qd, kd, val, kcd, at, gl = xs
        v_new = val - kcd @ state
        o = qd @ state + at @ v_new
        state = state * gl[..., None, None] + jnp.swapaxes(kd, -1, -2) @ v_new
        return state, o

    xs = (jnp.moveaxis(q_dec, 2, 0), jnp.moveaxis(k_dec, 2, 0), jnp.moveaxis(value, 2, 0), jnp.moveaxis(k_cumdecay, 2, 0), jnp.moveaxis(attn, 2, 0), jnp.moveaxis(g_last, 2, 0))
    s0 = jnp.zeros((B, H, Dk, Dv), jnp.float32)
    _, o = lax.scan(step, s0, xs)
    o = jnp.moveaxis(o, 0, 2).reshape(B, H, L, Dv)
    return jnp.moveaxis(o, 1, 2)


def gated_deltanet(h, w_in, conv_w, a_log, dt_bias, o_norm, w_out):
    B, L, _ = h.shape
    f32 = jnp.float32
    p = h @ w_in
    qkv, z, b, a = jnp.split(p, list(ODD_SPLITS), axis=-1)
    qkv = jax.nn.silu(causal_depthwise_conv(qkv, conv_w)).astype(f32)
    q, k, v = [t.reshape(B, L, C_HEADS, C_HEAD_DIM) for t in jnp.split(qkv, 3, axis=-1)]
    q = l2norm(q) * (C_HEAD_DIM ** -0.5)
    k = l2norm(k)
    beta = jax.nn.sigmoid(b.astype(f32))
    g = -jnp.exp(a_log.astype(f32)) * jax.nn.softplus(a.astype(f32) + dt_bias.astype(f32))
    o = chunk_gated_delta_rule(q, k, v, beta, g)
    o = rmsnorm(o, o_norm) * jax.nn.silu(z.reshape(B, L, C_HEADS, C_HEAD_DIM).astype(f32))
    return o.reshape(B, L, C_WIDTH).astype(h.dtype) @ w_out


def squared_relu_mlp(h, w1, w2):
    return jnp.square(jax.nn.relu(h @ w1)) @ w2


def setup_inputs(seed: int = 0):
    key = jax.random.key(seed)
    ks = jax.random.split(key, 20)
    f32 = jnp.float32

    def nrm(k, shape, fan_in):
        return jax.random.normal(k, shape, f32) * (fan_in ** -0.5)

    def gain(k, shape):
        return 1.0 + 0.02 * jax.random.normal(k, shape, f32)

    x = jax.random.normal(ks[0], (BATCH, SEQ, D_MODEL), f32)
    mix_norm = gain(ks[1], (DEPTH, D_MODEL))
    mlp_norm = gain(ks[2], (DEPTH, D_MODEL))
    w_ff1 = nrm(ks[3], (DEPTH, D_MODEL, D_FF), D_MODEL)
    w_ff2 = nrm(ks[4], (DEPTH, D_FF, D_MODEL), D_FF)
    ev_w_in = nrm(ks[5], (N_EVEN, D_MODEL, EVEN_IN), D_MODEL)
    ev_kv_norm = gain(ks[6], (N_EVEN, A_KV_RANK))
    ev_w_uk = nrm(ks[7], (N_EVEN, A_KV_RANK, A_HEAD_DIM), A_KV_RANK)
    ev_w_uv = nrm(ks[8], (N_EVEN, A_KV_RANK, A_HEAD_DIM), A_KV_RANK)
    ev_pool_w = nrm(ks[9], (N_EVEN, len(POOL_WINDOWS), POOL_GROUP, POOL_GROUP), POOL_GROUP)
    ev_pool_scale = 1.0 + 0.1 * jax.random.normal(ks[10], (N_EVEN, POOL_WIDTH), f32)
    ev_w_out = nrm(ks[11], (N_EVEN, D_MODEL, D_MODEL), D_MODEL)
    od_w_in = nrm(ks[12], (N_ODD, D_MODEL, ODD_IN), D_MODEL)
    od_conv_w = nrm(ks[13], (N_ODD, CONV_WIDTH, 3 * C_WIDTH), CONV_WIDTH)
    od_a_log = jnp.log(jax.random.uniform(ks[14], (N_ODD, C_HEADS), f32, 1.0, 16.0))
    dt = jnp.exp(jax.random.uniform(ks[15], (N_ODD, C_HEADS), f32, math.log(1e-3), math.log(1e-1)))
    od_dt_bias = dt + jnp.log(-jnp.expm1(-dt))
    od_o_norm = gain(ks[16], (N_ODD, C_HEAD_DIM))
    od_w_out = nrm(ks[17], (N_ODD, C_WIDTH, D_MODEL), C_WIDTH)
    final_norm = gain(ks[18], (D_MODEL,))
    return {'x': x, 'mix_norm': mix_norm, 'mlp_norm': mlp_norm, 'w_ff1': w_ff1, 'w_ff2': w_ff2,
            'ev_w_in': ev_w_in, 'ev_kv_norm': ev_kv_norm, 'ev_w_uk': ev_w_uk, 'ev_w_uv': ev_w_uv,
            'ev_pool_w': ev_pool_w, 'ev_pool_scale': ev_pool_scale, 'ev_w_out': ev_w_out,
            'od_w_in': od_w_in, 'od_conv_w': od_conv_w, 'od_a_log': od_a_log, 'od_dt_bias': od_dt_bias,
            'od_o_norm': od_o_norm, 'od_w_out': od_w_out, 'final_norm': final_norm}


def reference(x, mix_norm, mlp_norm, w_ff1, w_ff2, ev_w_in, ev_kv_norm, ev_w_uk, ev_w_uv, ev_pool_w, ev_pool_scale, ev_w_out, od_w_in, od_conv_w, od_a_log, od_dt_bias, od_o_norm, od_w_out, final_norm):
    h = x
    for layer in range(DEPTH):
        hn = rmsnorm(h, mix_norm[layer])
        j = layer // 2
        if layer % 2 == 0:
            h = h + sparse_pool_mixer(hn, ev_w_in[j], ev_kv_norm[j], ev_w_uk[j], ev_w_uv[j], ev_pool_w[j], ev_pool_scale[j], ev_w_out[j])
        else:
            h = h + gated_deltanet(hn, od_w_in[j], od_conv_w[j], od_a_log[j], od_dt_bias[j], od_o_norm[j], od_w_out[j])
        h = h + squared_relu_mlp(rmsnorm(h, mlp_norm[layer]), w_ff1[layer], w_ff2[layer])
    return rmsnorm(h, final_norm)
```

```python
import functools

import numpy as np
import jax
import jax.numpy as jnp
from jax import lax
from jax.experimental import pallas as pl
from jax.experimental.pallas import tpu as pltpu

F32 = jnp.float32
I32 = jnp.int32
MXU_DTYPE = jnp.bfloat16

A_HEADS = 8
A_HEAD_DIM = 64
A_WIDTH = A_HEADS * A_HEAD_DIM
A_KV_RANK = 128
IDX_HEADS = 8
IDX_DIM = 64
TOPK_MAX = 256
POOL_WINDOWS = (2, 4, 8, 16)
POOL_GROUP = 128
POOL_WIDTH = POOL_GROUP * len(POOL_WINDOWS)
C_HEADS = 8
C_HEAD_DIM = 128
C_WIDTH = C_HEADS * C_HEAD_DIM
CONV_WIDTH = 4
CHUNK = 64
ROPE_THETA = 10000.0
NORM_EPS = 1e-6

LANES = 128
INT_MIN = np.int32(-2 ** 31)
MASKED_LOGIT = -1e30

NT_DIMS = (((1,), (1,)), ((), ()))
TN_DIMS = (((0,), (0,)), ((), ()))


def _params(semantics, vmem_mb):
    return pltpu.CompilerParams(dimension_semantics=semantics, vmem_limit_bytes=vmem_mb * 1024 * 1024)


def _dot(a, b):
    return jnp.dot(a, b, preferred_element_type=F32)


def _dot_nt(a, b):
    return lax.dot_general(a, b, NT_DIMS, preferred_element_type=F32)


def _split(a):
    hi = a.astype(MXU_DTYPE)
    lo = (a - hi.astype(F32)).astype(MXU_DTYPE)
    return hi, lo


def _dot3(a, b):
    ah, al = _split(a)
    bh, bl = _split(b)
    return _dot(ah, bh) + (_dot(ah, bl) + _dot(al, bh))


def _rms(x, g):
    return x * lax.rsqrt(jnp.mean(x * x, axis=-1, keepdims=True) + NORM_EPS) * g


def _silu(x):
    return x * jax.nn.sigmoid(x)


def _resident(shape):
    nd = len(shape)
    return pl.BlockSpec(shape, lambda *_: (0,) * nd, pipeline_mode=pl.Buffered(1))


def _rope(x, cos, sin_signed, first_half):
    partner = jnp.where(first_half, pltpu.roll(x, LANES - 32, 1), pltpu.roll(x, 32, 1))
    return x * cos + partner * sin_signed


def _even_proj_kernel(h_ref, g_ref, w_ref, kvn_ref, wkv_ref, cos_ref, sin_ref, cosk_ref, sink_ref,
                      q_ref, qi_ref, u_ref, kv_ref, kib_ref, kiw_ref):
    tm = h_ref.shape[0]
    xn = _rms(h_ref[...], g_ref[...]).astype(MXU_DTYPE)
    p = _dot(xn, w_ref[...])
    lane = lax.broadcasted_iota(I32, (tm, LANES), 1)
    first_half = (lane % 64) < 32
    low = lane < 64
    cos, sin = cos_ref[...], sin_ref[...]
    cosk, sink = cosk_ref[...], sink_ref[...]

    def heads_out(o_ref, base, scale):
        for s in range(4):
            r = _rope(p[:, base + s * LANES: base + (s + 1) * LANES], cos, sin, first_half)
            if scale != 1.0:
                r = r * scale
            o_ref[2 * s] = jnp.where(low, r, 0.0).astype(o_ref.dtype)
            o_ref[2 * s + 1] = jnp.where(low, pltpu.roll(r, 64, 1), 0.0).astype(o_ref.dtype)

    heads_out(q_ref, 0, A_HEAD_DIM ** -0.5)
    heads_out(qi_ref, A_WIDTH, 1.0)
    u_ref[...] = p[:, 1024:1536]
    ckv = _rms(p[:, 1536:1664], kvn_ref[...]).astype(MXU_DTYPE)
    kv = _rope(_dot(ckv, wkv_ref[...]), cosk, sink, first_half)
    kv_ref[...] = kv.astype(kv_ref.dtype)
    kiw = _rope(p[:, 1664:1792], cosk, sink, first_half)
    is_w = jnp.logical_and(lane >= 64, lane < 64 + IDX_HEADS)
    kiw = jnp.where(is_w, (kiw * (IDX_HEADS ** -0.5)) * (IDX_DIM ** -0.5), kiw)
    kiw_ref[...] = kiw
    kib_ref[...] = kiw.astype(kib_ref.dtype)


def _even_proj(h2, g, w_all, kvn, wkv, tabs, L, tm):
    T, D = h2.shape
    n_pos = L // tm
    N = w_all.shape[1]
    row = lambda i: (i, 0)
    pos = lambda i: (i % n_pos, 0)
    tab_spec = pl.BlockSpec((tm, LANES), pos)
    head_spec = pl.BlockSpec((A_HEADS, tm, LANES), lambda i: (0, i, 0))
    return pl.pallas_call(
        _even_proj_kernel,
        grid=(T // tm,),
        in_specs=[pl.BlockSpec((tm, D), row), _resident((1, D)), _resident((D, N)), _resident((1, A_KV_RANK)),
                  _resident((A_KV_RANK, LANES)), tab_spec, tab_spec, tab_spec, tab_spec],
        out_specs=[head_spec, head_spec, pl.BlockSpec((tm, POOL_WIDTH), row), pl.BlockSpec((tm, LANES), row),
                   pl.BlockSpec((tm, LANES), row), pl.BlockSpec((tm, LANES), row)],
        out_shape=[jax.ShapeDtypeStruct((A_HEADS, T, LANES), MXU_DTYPE), jax.ShapeDtypeStruct((IDX_HEADS, T, LANES), MXU_DTYPE),
                   jax.ShapeDtypeStruct((T, POOL_WIDTH), F32), jax.ShapeDtypeStruct((T, LANES), MXU_DTYPE),
                   jax.ShapeDtypeStruct((T, LANES), MXU_DTYPE), jax.ShapeDtypeStruct((T, LANES), F32)],
        compiler_params=_params(("parallel",), 40),
        name="even_proj",
    )(h2, g, w_all, kvn, wkv, *tabs)


def _dsa_kernel(q_ref, qi_ref, kiw_ref, kv_ref, kib_ref, o_ref,
                keys_ref, s_ref, p_ref, m_ref, l_ref, acc_ref, *, tq, kc, topk, pos_bits):
    i = pl.program_id(1)
    n_chunks = ((i + 1) * tq + kc - 1) // kc
    nh = q_ref.shape[0]
    q_all = q_ref[...].reshape(nh * tq, LANES)
    qi_all = qi_ref[...].reshape(nh * tq, LANES)
    qpos = i * tq + lax.broadcasted_iota(I32, (tq, 1), 0)
    kpos0 = lax.broadcasted_iota(I32, (tq, kc), 1)
    wi = kiw_ref[...]
    wcols = [wi[:, 64 + h: 65 + h] for h in range(nh)]

    def score_chunk(c, carry):
        kic = kib_ref[pl.ds(pl.multiple_of(c * kc, kc), kc), :]
        s_ref[...] = _dot_nt(qi_all, kic)
        score = jnp.zeros((tq, kc), F32)
        for h in range(nh):
            score = score + jnp.maximum(s_ref[h * tq:(h + 1) * tq, :], 0.0) * wcols[h]
        bits = lax.bitcast_convert_type(score, I32)
        mag = bits & np.int32(0x7FFFFFFF)
        key = jnp.where(bits < 0, -mag, mag)
        keys_ref[c] = jnp.where(c * kc + kpos0 <= qpos, key, INT_MIN)
        return carry

    lax.fori_loop(0, n_chunks, score_chunk, 0)

    def count(pred):
        def body(c, acc):
            w = jnp.where(pred(keys_ref[c], c), 1, 0)
            for j in range(kc // LANES):
                acc = acc + w[:, j * LANES:(j + 1) * LANES]
            return acc
        acc = lax.fori_loop(0, n_chunks, body, jnp.zeros((tq, LANES), I32))
        return jnp.sum(acc, axis=1, keepdims=True)

    k_eff = jnp.minimum(qpos + 1, topk)

    def thr_bit(it, thr):
        cand = thr ^ lax.shift_left(np.int32(1), 31 - it)
        cnt = count(lambda kb, c: kb >= cand)
        return jnp.where(cnt >= k_eff, cand, thr)

    thr = lax.fori_loop(0, 32, thr_bit, jnp.full((tq, 1), INT_MIN, I32))

    need = k_eff - count(lambda kb, c: kb > thr)
    n_ge = count(lambda kb, c: kb >= thr)
    has_tie = jnp.max(n_ge - k_eff) > 0

    def cut_bit(it, cut):
        cand = cut | lax.shift_left(np.int32(1), pos_bits - 1 - it)
        cnt = count(lambda kb, c: jnp.logical_and(kb == thr, c * kc + kpos0 < cand))
        return jnp.where(cnt < need, cand, cut)

    cut = lax.cond(has_tie,
                   lambda: lax.fori_loop(0, pos_bits, cut_bit, jnp.zeros((tq, 1), I32)),
                   lambda: jnp.full((tq, 1), 2 ** pos_bits - 1, I32))

    m_ref[...] = jnp.full(m_ref.shape, MASKED_LOGIT, F32)
    l_ref[...] = jnp.zeros(l_ref.shape, F32)
    acc_ref[...] = jnp.zeros(acc_ref.shape, F32)

    def attend_chunk(c, carry):
        kvc = kv_ref[pl.ds(pl.multiple_of(c * kc, kc), kc), :]
        kb = keys_ref[c]
        sel = jnp.logical_or(kb > thr, jnp.logical_and(kb == thr, c * kc + kpos0 <= cut))
        s_ref[...] = _dot_nt(q_all, kvc)
        for h in range(nh):
            rows = slice(h * tq, (h + 1) * tq)
            s = jnp.where(sel, s_ref[rows, :], MASKED_LOGIT)
            m_old = m_ref[rows, :]
            m_new = jnp.maximum(m_old, jnp.max(s, axis=1, keepdims=True))
            alpha = jnp.exp(m_old - m_new)
            p = jnp.exp(s - m_new)
            l_ref[rows, :] = alpha * l_ref[rows, :] + jnp.sum(p, axis=1, keepdims=True)
            m_ref[rows, :] = m_new
            p_ref[rows, :] = p.astype(p_ref.dtype)
            acc_ref[rows, :] = acc_ref[rows, :] * alpha
        acc_ref[...] += _dot(p_ref[...], kvc)
        return carry

    lax.fori_loop(0, n_chunks, attend_chunk, 0)

    low = lax.broadcasted_iota(I32, (tq, LANES), 1) < 64
    for s in range(nh // 2):
        r0 = slice(2 * s * tq, (2 * s + 1) * tq)
        r1 = slice((2 * s + 1) * tq, (2 * s + 2) * tq)
        a0 = acc_ref[r0, :] / l_ref[r0, :]
        a1 = acc_ref[r1, :] / l_ref[r1, :]
        o_ref[:, s * LANES:(s + 1) * LANES] = jnp.where(low, pltpu.roll(a0, 64, 1), a1).astype(o_ref.dtype)


def _dsa(qh, qih, kiw, kv, kib, B, L, tq, kc):
    T = B * L
    nq = L // tq
    topk = min(TOPK_MAX, L // 4)
    pos_bits = max(1, int(np.ceil(np.log2(L))))
    head_spec = pl.BlockSpec((A_HEADS, tq, LANES), lambda b, i: (0, b * nq + i, 0))
    row_spec = pl.BlockSpec((tq, LANES), lambda b, i: (b * nq + i, 0))
    seq_spec = pl.BlockSpec((L, LANES), lambda b, i: (b, 0))
    kern = functools.partial(_dsa_kernel, tq=tq, kc=kc, topk=topk, pos_bits=pos_bits)
    return pl.pallas_call(
        kern,
        grid=(B, nq),
        in_specs=[head_spec, head_spec, row_spec, seq_spec, seq_spec],
        out_specs=pl.BlockSpec((tq, A_WIDTH), lambda b, i: (b * nq + i, 0)),
        out_shape=jax.ShapeDtypeStruct((T, A_WIDTH), MXU_DTYPE),
        scratch_shapes=[pltpu.VMEM((L // kc, tq, kc), I32), pltpu.VMEM((A_HEADS * tq, kc), F32),
                        pltpu.VMEM((A_HEADS * tq, kc), MXU_DTYPE), pltpu.VMEM((A_HEADS * tq, 1), F32),
                        pltpu.VMEM((A_HEADS * tq, 1), F32), pltpu.VMEM((A_HEADS * tq, LANES), F32)],
        compiler_params=_params(("parallel", "arbitrary"), 40),
        name="dsa",
    )(qh, qih, kiw, kv, kib)


POOL_HALO = 16


def _pool_kernel(ya_ref, u_ref, halo_ref, pw_ref, ps_ref, o_ref, *, n_pos):
    tm = u_ref.shape[0]
    i = pl.program_id(0)
    seq_start = (i % n_pos) == 0
    u = u_ref[...]
    halo = jnp.where(seq_start, 0.0, halo_ref[...])
    ue = jnp.concatenate([halo, u], axis=0)
    t = (i % n_pos) * tm + lax.broadcasted_iota(I32, (tm, 1), 0)
    o_ref[:, :A_WIDTH] = ya_ref[...]
    for gi, w in enumerate(POOL_WINDOWS):
        lanes = slice(gi * POOL_GROUP, (gi + 1) * POOL_GROUP)
        s = ue[:, lanes]
        span = 1
        while span < w:
            s = s + pltpu.roll(s, span, 0)
            span *= 2
        count = jnp.minimum(t + 1, w).astype(F32)
        pooled = s[POOL_HALO:, :] / count - u[:, lanes]
        y = _dot(pooled.astype(MXU_DTYPE), pw_ref[gi]) * ps_ref[:, lanes]
        o_ref[:, A_WIDTH + gi * POOL_GROUP: A_WIDTH + (gi + 1) * POOL_GROUP] = y.astype(o_ref.dtype)


def _pool(ya, u, pool_w, pool_scale, L, tm):
    T = u.shape[0]
    n_pos = L // tm
    row = lambda i: (i, 0)
    halo_map = lambda i: (jnp.maximum(i * (tm // POOL_HALO) - 1, 0), 0)
    return pl.pallas_call(
        functools.partial(_pool_kernel, n_pos=n_pos),
        grid=(T // tm,),
        in_specs=[pl.BlockSpec((tm, A_WIDTH), row), pl.BlockSpec((tm, POOL_WIDTH), row),
                  pl.BlockSpec((POOL_HALO, POOL_WIDTH), halo_map),
                  _resident(pool_w.shape), _resident((1, POOL_WIDTH))],
        out_specs=pl.BlockSpec((tm, A_WIDTH + POOL_WIDTH), row),
        out_shape=jax.ShapeDtypeStruct((T, A_WIDTH + POOL_WIDTH), MXU_DTYPE),
        compiler_params=_params(("parallel",), 32),
        name="pool",
    )(ya, u, u, pool_w, pool_scale)


def _odd_proj_kernel(h_ref, g_ref, w_ref, qkv_ref, z_ref, ba_ref):
    xn = _rms(h_ref[...], g_ref[...]).astype(MXU_DTYPE)
    n_qkv = qkv_ref.shape[1]
    n_z = z_ref.shape[1]
    step = 512
    for c in range(n_qkv // step):
        qkv_ref[:, c * step:(c + 1) * step] = _dot(xn, w_ref[:, c * step:(c + 1) * step])
    for c in range(n_z // step):
        z_ref[:, c * step:(c + 1) * step] = _dot(xn, w_ref[:, n_qkv + c * step: n_qkv + (c + 1) * step])
    ba_ref[...] = _dot(xn, w_ref[:, n_qkv + n_z:])


def _odd_proj(h2, g, w_all, tm):
    T, D = h2.shape
    N = w_all.shape[1]
    row = lambda i: (i, 0)
    return pl.pallas_call(
        _odd_proj_kernel,
        grid=(T // tm,),
        in_specs=[pl.BlockSpec((tm, D), row), _resident((1, D)), _resident((D, N))],
        out_specs=[pl.BlockSpec((tm, 3 * C_WIDTH), row), pl.BlockSpec((tm, C_WIDTH), row), pl.BlockSpec((tm, LANES), row)],
        out_shape=[jax.ShapeDtypeStruct((T, 3 * C_WIDTH), F32), jax.ShapeDtypeStruct((T, C_WIDTH), F32),
                   jax.ShapeDtypeStruct((T, LANES), F32)],
        compiler_params=_params(("parallel",), 48),
        name="odd_proj",
    )(h2, g, w_all)


def _gates_kernel(ba_ref, alog_ref, dtb_ref, gate_ref, cum_ref):
    tm = ba_ref.shape[0]
    x = ba_ref[...]
    lane = lax.broadcasted_iota(I32, x.shape, 1)
    beta = jax.nn.sigmoid(x)
    g = -jnp.exp(alog_ref[...]) * jnp.logaddexp(x + dtb_ref[...], 0.0)
    gate = jnp.where(lane < C_HEADS, beta, jnp.where(lane < 2 * C_HEADS, g, 0.0))
    gate_ref[...] = gate
    ii = lax.broadcasted_iota(I32, (CHUNK, CHUNK), 0)
    jj = lax.broadcasted_iota(I32, (CHUNK, CHUNK), 1)
    tri = (ii >= jj).astype(F32)
    for c in range(tm // CHUNK):
        rows = slice(c * CHUNK, (c + 1) * CHUNK)
        cum_ref[rows, :] = _dot3(tri, gate[rows, :])


def _gates(ba, alog_row, dtb_row, tm):
    T = ba.shape[0]
    row = lambda i: (i, 0)
    return pl.pallas_call(
        _gates_kernel,
        grid=(T // tm,),
        in_specs=[pl.BlockSpec((tm, LANES), row), _resident((1, LANES)), _resident((1, LANES))],
        out_specs=[pl.BlockSpec((tm, LANES), row), pl.BlockSpec((tm, LANES), row)],
        out_shape=[jax.ShapeDtypeStruct((T, LANES), F32), jax.ShapeDtypeStruct((T, LANES), F32)],
        compiler_params=_params(("parallel",), 16),
        name="gates",
    )(ba, alog_row, dtb_row)


CONV_HALO = 8


def _gdn_kernel(xq_ref, xk_ref, xv_ref, hq_ref, hk_ref, hv_ref, wq_ref, wk_ref, wv_ref,
                gcol_ref, grow_ref, z_ref, on_ref, y_ref, state_ref):
    rows = xq_ref.shape[0]
    seq_start = pl.program_id(2) == 0

    @pl.when(seq_start)
    def _():
        state_ref[...] = jnp.zeros(state_ref.shape, F32)

    def conv_silu(x_ref, halo_ref, w_ref):
        x = x_ref[...]
        halo = jnp.where(seq_start, 0.0, halo_ref[...])
        xe = jnp.concatenate([halo, x], axis=0)
        w = w_ref[...]
        y = w[CONV_WIDTH - 1:CONV_WIDTH, :] * x
        for j in range(CONV_WIDTH - 1):
            back = CONV_WIDTH - 1 - j
            y = y + w[j:j + 1, :] * pltpu.roll(xe, back, 0)[CONV_HALO:, :]
        return _silu(y)

    def l2n(x):
        return x * lax.rsqrt(jnp.sum(x * x, axis=-1, keepdims=True) + NORM_EPS)

    q = l2n(conv_silu(xq_ref, hq_ref, wq_ref)) * (C_HEAD_DIM ** -0.5)
    k = l2n(conv_silu(xk_ref, hk_ref, wk_ref))
    v = conv_silu(xv_ref, hv_ref, wv_ref)
    gcol = gcol_ref[0]
    grow = grow_ref[0, 0]
    z = z_ref[...]

    ii = lax.broadcasted_iota(I32, (CHUNK, CHUNK), 0)
    jj = lax.broadcasted_iota(I32, (CHUNK, CHUNK), 1)
    tril = ii >= jj
    strict = ii > jj
    eye = (ii == jj).astype(F32)

    for c in range(rows // CHUNK):
        sl = slice(c * CHUNK, (c + 1) * CHUNK)
        qc, kc, vc = q[sl], k[sl], v[sl]
        beta = gcol[sl, 0:1]
        gc = gcol[sl, 1:2]
        gc_row = grow[1:2, c * CHUNK:(c + 1) * CHUNK]
        g_last = gc[CHUNK - 1:CHUNK, :]
        decay = jnp.exp(jnp.where(tril, gc - gc_row, -jnp.inf))
        kb = kc.astype(MXU_DTYPE)
        lmat = jnp.where(strict, beta * _dot_nt(kb, kb) * decay, 0.0)
        a = -lmat
        tmat = eye + a
        span = 2
        while span < CHUNK:
            a = _dot3(a, a)
            tmat = tmat + _dot3(tmat, a)
            span *= 2
        rhs = jnp.concatenate([vc * beta, kc * (beta * jnp.exp(gc))], axis=1).astype(MXU_DTYPE)
        solved = _dot(tmat.astype(MXU_DTYPE), rhs)
        value, k_cumdecay = solved[:, :C_HEAD_DIM], solved[:, C_HEAD_DIM:]
        attn = _dot_nt(qc.astype(MXU_DTYPE), kb) * decay
        q_dec = (qc * jnp.exp(gc)).astype(MXU_DTYPE)
        k_dec = (kc * jnp.exp(g_last - gc)).astype(MXU_DTYPE)

        state = state_ref[...]
        both = _dot(jnp.concatenate([k_cumdecay.astype(MXU_DTYPE), q_dec], axis=0), state.astype(MXU_DTYPE))
        v_new = value - both[:CHUNK]
        v_new_b = v_new.astype(MXU_DTYPE)
        o = both[CHUNK:] + _dot(attn.astype(MXU_DTYPE), v_new_b)
        state_ref[...] = state * jnp.exp(g_last) + lax.dot_general(k_dec, v_new_b, TN_DIMS, preferred_element_type=F32)

        y = _rms(o, on_ref[...]) * _silu(z[sl])
        y_ref[sl, :] = y.astype(y_ref.dtype)


def _gdn(qkv, conv_w, gcol, grow, z, o_norm, B, L, rows):
    T = B * L
    H = C_HEADS
    nr = L // rows
    hb = rows // CONV_HALO

    def x_spec(part):
        return pl.BlockSpec((rows, C_HEAD_DIM), lambda b, h, r: (b * nr + r, part * H + h))

    def halo_spec(part):
        return pl.BlockSpec((CONV_HALO, C_HEAD_DIM), lambda b, h, r: (jnp.maximum((b * nr + r) * hb - 1, 0), part * H + h))

    def w_spec(part):
        return pl.BlockSpec((CONV_WIDTH, C_HEAD_DIM), lambda b, h, r: (0, part * H + h))

    return pl.pallas_call(
        _gdn_kernel,
        grid=(B, H, nr),
        in_specs=[x_spec(0), x_spec(1), x_spec(2), halo_spec(0), halo_spec(1), halo_spec(2),
                  w_spec(0), w_spec(1), w_spec(2),
                  pl.BlockSpec((1, rows, 2), lambda b, h, r: (h, b * nr + r, 0)),
                  pl.BlockSpec((1, 1, 2, rows), lambda b, h, r: (b, h, 0, r)),
                  pl.BlockSpec((rows, C_HEAD_DIM), lambda b, h, r: (b * nr + r, h)),
                  pl.BlockSpec((1, C_HEAD_DIM), lambda b, h, r: (0, 0))],
        out_specs=pl.BlockSpec((rows, C_HEAD_DIM), lambda b, h, r: (b * nr + r, h)),
        out_shape=jax.ShapeDtypeStruct((T, C_WIDTH), MXU_DTYPE),
        scratch_shapes=[pltpu.VMEM((C_HEAD_DIM, C_HEAD_DIM), F32)],
        compiler_params=_params(("parallel", "parallel", "arbitrary"), 32),
        name="gdn",
    )(qkv, qkv, qkv, qkv, qkv, qkv, conv_w, conv_w, conv_w, gcol, grow, z, o_norm)


def _out_mlp_kernel(h_ref, y_ref, wo_ref, g_ref, w1_ref, w2_ref, fg_ref, o_ref, *, ff_chunk, final):
    h1 = h_ref[...] + _dot(y_ref[...], wo_ref[...])
    xn = _rms(h1, g_ref[...]).astype(MXU_DTYPE)
    acc = h1
    for c in range(w1_ref.shape[1] // ff_chunk):
        cols = slice(c * ff_chunk, (c + 1) * ff_chunk)
        a = jnp.square(jnp.maximum(_dot(xn, w1_ref[:, cols]), 0.0)).astype(MXU_DTYPE)
        acc = acc + _dot(a, w2_ref[cols, :])
    if final:
        acc = _rms(acc, fg_ref[...])
    o_ref[...] = acc


def _out_mlp(h2, y, w_out, g, w1, w2, final_g, final, tm):
    T, D = h2.shape
    dff = w1.shape[1]
    row = lambda i: (i, 0)
    return pl.pallas_call(
        functools.partial(_out_mlp_kernel, ff_chunk=512, final=final),
        grid=(T // tm,),
        in_specs=[pl.BlockSpec((tm, D), row), pl.BlockSpec((tm, y.shape[1]), row), _resident(w_out.shape),
                  _resident((1, D)), _resident((D, dff)), _resident((dff, D)), _resident((1, D))],
        out_specs=pl.BlockSpec((tm, D), row),
        out_shape=jax.ShapeDtypeStruct((T, D), F32),
        compiler_params=_params(("parallel",), 48),
        name="out_mlp",
    )(h2, y, w_out, g, w1, w2, final_g)


def _rope_tables(L):
    inv = ROPE_THETA ** (-jnp.arange(0, A_HEAD_DIM, 2, dtype=F32) / A_HEAD_DIM)
    ang = jnp.arange(L, dtype=F32)[:, None] * inv[None, :]
    cos, sin = jnp.cos(ang), jnp.sin(ang)
    one, zero = jnp.ones((L, 64), F32), jnp.zeros((L, 64), F32)
    cos2 = jnp.concatenate([cos, cos, cos, cos], axis=1)
    sin2 = jnp.concatenate([-sin, sin, -sin, sin], axis=1)
    cosk = jnp.concatenate([cos, cos, one], axis=1)
    sink = jnp.concatenate([-sin, sin, zero], axis=1)
    return cos2, sin2, cosk, sink


def _pick_tile(L, want):
    t = min(want, L)
    while L % t:
        t //= 2
    return t


def kernel(x, mix_norm, mlp_norm, w_ff1, w_ff2, ev_w_in, ev_kv_norm, ev_w_uk, ev_w_uv, ev_pool_w, ev_pool_scale, ev_w_out, od_w_in, od_conv_w, od_a_log, od_dt_bias, od_o_norm, od_w_out, final_norm):
    B, L, D = x.shape
    T = B * L
    depth = mix_norm.shape[0]
    mm = MXU_DTYPE
    h = x.reshape(T, D)
    tabs = _rope_tables(L)
    tm = _pick_tile(L, 256)
    tm_mlp = _pick_tile(L, 512)
    tq = _pick_tile(L, 128)
    kc = _pick_tile(L, 256)
    gdn_rows = _pick_tile(L, 256)
    fg = final_norm.reshape(1, D)

    for layer in range(depth):
        j = layer // 2
        g_mix = mix_norm[layer].reshape(1, D)
        if layer % 2 == 0:
            w = ev_w_in[j]
            c0 = A_WIDTH
            c1 = c0 + A_KV_RANK
            c2 = c1 + IDX_HEADS * IDX_DIM
            c3 = c2 + IDX_DIM
            c4 = c3 + IDX_HEADS
            pad = jnp.zeros((D, LANES - IDX_DIM - IDX_HEADS), w.dtype)
            w_all = jnp.concatenate([w[:, :c0], w[:, c1:c2], w[:, c4:], w[:, c0:c1], w[:, c2:c3], w[:, c3:c4], pad], axis=1).astype(mm)
            wkv = jnp.concatenate([ev_w_uk[j], ev_w_uv[j]], axis=1).astype(mm)
            qh, qih, u, kv, kib, kiw = _even_proj(h, g_mix, w_all, ev_kv_norm[j].reshape(1, -1), wkv, tabs, L, tm)
            ya = _dsa(qh, qih, kiw, kv, kib, B, L, tq, kc)
            y = _pool(ya, u, ev_pool_w[j].astype(mm), ev_pool_scale[j].reshape(1, -1), L, tm)
            w_out = ev_w_out[j].astype(mm)
        else:
            w = od_w_in[j]
            n_main = 4 * C_WIDTH
            pad = jnp.zeros((D, LANES - 2 * C_HEADS), w.dtype)
            w_all = jnp.concatenate([w, pad], axis=1).astype(mm)
            qkv, z, ba = _odd_proj(h, g_mix, w_all, tm)
            zero8 = jnp.zeros((C_HEADS,), F32)
            lane_pad = jnp.zeros((LANES - 2 * C_HEADS,), F32)
            alog_row = jnp.concatenate([zero8, od_a_log[j], lane_pad]).reshape(1, LANES)
            dtb_row = jnp.concatenate([zero8, od_dt_bias[j], lane_pad]).reshape(1, LANES)
            gate, cum = _gates(ba, alog_row, dtb_row, tm)
            beta = gate[:, :C_HEADS]
            gcum = cum[:, C_HEADS:2 * C_HEADS]
            gcol = jnp.stack([beta.T, gcum.T], axis=-1)
            grow = jnp.stack([beta.reshape(B, L, C_HEADS), gcum.reshape(B, L, C_HEADS)], axis=1)
            grow = jnp.transpose(grow, (0, 3, 1, 2))
            y = _gdn(qkv, od_conv_w[j], gcol, grow, z, od_o_norm[j].reshape(1, -1), B, L, gdn_rows)
            w_out = od_w_out[j].astype(mm)
        h = _out_mlp(h, y, w_out, mlp_norm[layer].reshape(1, D), w_ff1[layer].astype(mm), w_ff2[layer].astype(mm),
                     fg, layer == depth - 1, tm_mlp)
    return h.reshape(B, L, D)
```

```python
import functools

import numpy as np
import jax
import jax.numpy as jnp
from jax import lax
from jax.experimental import pallas as pl
from jax.experimental.pallas import tpu as pltpu

F32 = jnp.float32
I32 = jnp.int32
MXU_DTYPE = jnp.bfloat16

A_HEADS = 8
A_HEAD_DIM = 64
A_WIDTH = A_HEADS * A_HEAD_DIM
A_KV_RANK = 128
IDX_HEADS = 8
IDX_DIM = 64
TOPK_MAX = 256
POOL_WINDOWS = (2, 4, 8, 16)
POOL_GROUP = 128
POOL_WIDTH = POOL_GROUP * len(POOL_WINDOWS)
C_HEADS = 8
C_HEAD_DIM = 128
C_WIDTH = C_HEADS * C_HEAD_DIM
CONV_WIDTH = 4
CHUNK = 64
ROPE_THETA = 10000.0
NORM_EPS = 1e-6

LANES = 128
INT_MIN = np.int32(-2 ** 31)
MASKED_LOGIT = -1e30

NT_DIMS = (((1,), (1,)), ((), ()))
TN_DIMS = (((0,), (0,)), ((), ()))


def _params(semantics, vmem_mb):
    return pltpu.CompilerParams(dimension_semantics=semantics, vmem_limit_bytes=vmem_mb * 1024 * 1024)


def _dot(a, b):
    return jnp.dot(a, b, preferred_element_type=F32)


def _dot_nt(a, b):
    return lax.dot_general(a, b, NT_DIMS, preferred_element_type=F32)


def _split(a):
    hi = a.astype(MXU_DTYPE)
    lo = (a - hi.astype(F32)).astype(MXU_DTYPE)
    return hi, lo


def _dot3(a, b):
    ah, al = _split(a)
    bh, bl = _split(b)
    return _dot(ah, bh) + (_dot(ah, bl) + _dot(al, bh))


def _rms(x, g):
    return x * lax.rsqrt(jnp.mean(x * x, axis=-1, keepdims=True) + NORM_EPS) * g


def _silu(x):
    return x * jax.nn.sigmoid(x)


def _resident(shape):
    nd = len(shape)
    return pl.BlockSpec(shape, lambda *_: (0,) * nd, pipeline_mode=pl.Buffered(1))


def _rope(x, cos, sin_signed, first_half):
    partner = jnp.where(first_half, pltpu.roll(x, LANES - 32, 1), pltpu.roll(x, 32, 1))
    return x * cos + partner * sin_signed


def _even_proj_kernel(h_ref, g_ref, w_ref, kvn_ref, wkv_ref, cos_ref, sin_ref, cosk_ref, sink_ref,
                      qt_ref, qit_ref, u_ref, kv_ref, vt_ref, kib_ref, wit_ref, *, tq, kc):
    tm = h_ref.shape[0]
    xn = _rms(h_ref[...], g_ref[...]).astype(MXU_DTYPE)
    p = _dot(xn, w_ref[...])
    lane = lax.broadcasted_iota(I32, (tm, LANES), 1)
    first_half = (lane % 64) < 32
    cos, sin = cos_ref[...], sin_ref[...]
    cosk, sink = cosk_ref[...], sink_ref[...]

    def heads_out(o_ref, base, scale):
        for s in range(4):
            r = _rope(p[:, base + s * LANES: base + (s + 1) * LANES], cos, sin, first_half)
            if scale != 1.0:
                r = r * scale
            for j in range(tm // tq):
                rt = r[j * tq:(j + 1) * tq, :].T.astype(o_ref.dtype)
                o_ref[j, :, 2 * s * tq:(2 * s + 1) * tq] = rt[:64, :]
                o_ref[j, :, (2 * s + 1) * tq:(2 * s + 2) * tq] = rt[64:, :]

    heads_out(qt_ref, 0, A_HEAD_DIM ** -0.5)
    heads_out(qit_ref, A_WIDTH, 1.0)
    u_ref[...] = p[:, 1024:1536]
    ckv = _rms(p[:, 1536:1664], kvn_ref[...]).astype(MXU_DTYPE)
    kv = _rope(_dot(ckv, wkv_ref[...]), cosk, sink, first_half)
    kv_ref[...] = kv.astype(kv_ref.dtype)
    for j in range(tm // kc):
        vt_ref[j] = kv[j * kc:(j + 1) * kc, :].T[64:, :].astype(vt_ref.dtype)
    kiw = _rope(p[:, 1664:1792], cosk, sink, first_half)
    kib_ref[...] = kiw.astype(kib_ref.dtype)
    wit = kiw.T[64:64 + IDX_HEADS, :]
    wit_ref[...] = (wit * (IDX_HEADS ** -0.5)) * (IDX_DIM ** -0.5)


def _even_proj(h2, g, w_all, kvn, wkv, tabs, L, tm, tq, kc):
    T, D = h2.shape
    n_pos = L // tm
    N = w_all.shape[1]
    row = lambda i: (i, 0)
    pos = lambda i: (i % n_pos, 0)
    tab_spec = pl.BlockSpec((tm, LANES), pos)
    head_spec = pl.BlockSpec((tm // tq, 64, A_HEADS * tq), lambda i: (i, 0, 0))
    head_shape = jax.ShapeDtypeStruct((T // tq, 64, A_HEADS * tq), MXU_DTYPE)
    return pl.pallas_call(
        functools.partial(_even_proj_kernel, tq=tq, kc=kc),
        grid=(T // tm,),
        in_specs=[pl.BlockSpec((tm, D), row), _resident((1, D)), _resident((D, N)), _resident((1, A_KV_RANK)),
                  _resident((A_KV_RANK, LANES)), tab_spec, tab_spec, tab_spec, tab_spec],
        out_specs=[head_spec, head_spec, pl.BlockSpec((tm, POOL_WIDTH), row), pl.BlockSpec((tm, LANES), row),
                   pl.BlockSpec((tm // kc, 64, kc), lambda i: (i, 0, 0)), pl.BlockSpec((tm, LANES), row),
                   pl.BlockSpec((IDX_HEADS, tm), lambda i: (0, i))],
        out_shape=[head_shape, head_shape, jax.ShapeDtypeStruct((T, POOL_WIDTH), F32),
                   jax.ShapeDtypeStruct((T, LANES), MXU_DTYPE), jax.ShapeDtypeStruct((T // kc, 64, kc), MXU_DTYPE),
                   jax.ShapeDtypeStruct((T, LANES), MXU_DTYPE), jax.ShapeDtypeStruct((IDX_HEADS, T), F32)],
        compiler_params=_params(("parallel",), 40),
        name="even_proj",
    )(h2, g, w_all, kvn, wkv, *tabs)


def _dsa_kernel(qt_ref, qit_ref, wit_ref, kv_ref, vt_ref, kib_ref, o_ref,
                keys_ref, s_ref, p_ref, cap_ref, m_ref, l_ref, acc_ref, *, tq, kc, topk, pos_bits):
    i = pl.program_id(1)
    n_chunks = ((i + 1) * tq + kc - 1) // kc
    nh = A_HEADS
    qpos = i * tq + lax.broadcasted_iota(I32, (1, tq), 1)
    kpos0 = lax.broadcasted_iota(I32, (kc, tq), 0)
    wit = wit_ref[...]

    def score_chunk(c, carry):
        kic = kib_ref[pl.ds(pl.multiple_of(c * kc, kc), kc), :]
        s_ref[...] = _dot(kic[:, :IDX_DIM], qit_ref[0])
        score = jnp.zeros((kc, tq), F32)
        for h in range(nh):
            score = score + jnp.maximum(s_ref[:, h * tq:(h + 1) * tq], 0.0) * wit[h:h + 1, :]
        bits = lax.bitcast_convert_type(score, I32)
        mag = bits & np.int32(0x7FFFFFFF)
        key = jnp.where(bits < 0, -mag, mag)
        keys_ref[c] = jnp.where(c * kc + kpos0 <= qpos, key, INT_MIN)
        return carry

    lax.fori_loop(0, n_chunks, score_chunk, 0)

    def count(pred):
        def body(c, acc):
            w = jnp.where(pred(keys_ref[c], c), 1, 0)
            return acc + jnp.sum(w.reshape(kc // 8, 8, tq), axis=0)
        acc = lax.fori_loop(0, n_chunks, body, jnp.zeros((8, tq), I32))
        return jnp.sum(acc, axis=0, keepdims=True)

    k_eff = jnp.minimum(qpos + 1, topk)

    def thr_bit(it, thr):
        cand = thr ^ lax.shift_left(np.int32(1), 31 - it)
        cnt = count(lambda kb, c: kb >= cand)
        return jnp.where(cnt >= k_eff, cand, thr)

    thr = lax.fori_loop(0, 32, thr_bit, jnp.full((1, tq), INT_MIN, I32))

    need = k_eff - count(lambda kb, c: kb > thr)
    n_ge = count(lambda kb, c: kb >= thr)
    has_tie = jnp.max(n_ge - k_eff) > 0

    def cut_bit(it, cut):
        cand = cut | lax.shift_left(np.int32(1), pos_bits - 1 - it)
        cnt = count(lambda kb, c: jnp.logical_and(kb == thr, c * kc + kpos0 < cand))
        return jnp.where(cnt < need, cand, cut)

    cut = lax.cond(has_tie,
                   lambda: lax.fori_loop(0, pos_bits, cut_bit, jnp.zeros((1, tq), I32)),
                   lambda: jnp.full((1, tq), 2 ** pos_bits - 1, I32))

    m_ref[...] = jnp.full(m_ref.shape, MASKED_LOGIT, F32)
    l_ref[...] = jnp.zeros(l_ref.shape, F32)
    acc_ref[...] = jnp.zeros(acc_ref.shape, F32)

    def attend_chunk(c, carry):
        kvc = kv_ref[pl.ds(pl.multiple_of(c * kc, kc), kc), :]
        kb = keys_ref[c]
        sel = jnp.logical_or(kb > thr, jnp.logical_and(kb == thr, c * kc + kpos0 <= cut))
        cap_ref[...] = jnp.where(sel, jnp.inf, MASKED_LOGIT)
        s_ref[...] = _dot(kvc[:, :A_HEAD_DIM], qt_ref[0])
        for h in range(nh):
            cols = slice(h * tq, (h + 1) * tq)
            s = jnp.minimum(s_ref[:, cols], cap_ref[...])
            m_old = m_ref[:, cols]
            m_new = jnp.maximum(m_old, jnp.max(s, axis=0, keepdims=True))
            alpha = jnp.exp(m_old - m_new)
            p = jnp.exp(s - m_new)
            l_ref[:, cols] = alpha * l_ref[:, cols] + jnp.sum(p, axis=0, keepdims=True)
            m_ref[:, cols] = m_new
            p_ref[:, cols] = p.astype(p_ref.dtype)
            acc_ref[:, cols] = acc_ref[:, cols] * alpha
        acc_ref[...] += _dot(vt_ref[c], p_ref[...])
        return carry

    lax.fori_loop(0, n_chunks, attend_chunk, 0)

    out_t = jnp.concatenate([acc_ref[:, h * tq:(h + 1) * tq] / l_ref[:, h * tq:(h + 1) * tq] for h in range(nh)], axis=0)
    o_ref[...] = out_t.T.astype(o_ref.dtype)


def _dsa(qt, qit, wit, kv, vt, kib, B, L, tq, kc):
    T = B * L
    nq = L // tq
    nkc = L // kc
    topk = min(TOPK_MAX, L // 4)
    pos_bits = max(1, int(np.ceil(np.log2(L))))
    head_spec = pl.BlockSpec((1, 64, A_HEADS * tq), lambda b, i: (b * nq + i, 0, 0))
    seq_spec = pl.BlockSpec((L, LANES), lambda b, i: (b, 0))
    kern = functools.partial(_dsa_kernel, tq=tq, kc=kc, topk=topk, pos_bits=pos_bits)
    return pl.pallas_call(
        kern,
        grid=(B, nq),
        in_specs=[head_spec, head_spec, pl.BlockSpec((IDX_HEADS, tq), lambda b, i: (0, b * nq + i)),
                  seq_spec, pl.BlockSpec((nkc, 64, kc), lambda b, i: (b, 0, 0)), seq_spec],
        out_specs=pl.BlockSpec((tq, A_WIDTH), lambda b, i: (b * nq + i, 0)),
        out_shape=jax.ShapeDtypeStruct((T, A_WIDTH), MXU_DTYPE),
        scratch_shapes=[pltpu.VMEM((nkc, kc, tq), I32), pltpu.VMEM((kc, A_HEADS * tq), F32),
                        pltpu.VMEM((kc, A_HEADS * tq), MXU_DTYPE), pltpu.VMEM((kc, tq), F32),
                        pltpu.VMEM((1, A_HEADS * tq), F32), pltpu.VMEM((1, A_HEADS * tq), F32),
                        pltpu.VMEM((A_HEAD_DIM, A_HEADS * tq), F32)],
        compiler_params=_params(("parallel", "arbitrary"), 40),
        name="dsa",
    )(qt, qit, wit, kv, vt, kib)


POOL_HALO = 16


def _pool_kernel(ya_ref, u_ref, halo_ref, pw_ref, ps_ref, o_ref, *, n_pos):
    tm = u_ref.shape[0]
    i = pl.program_id(0)
    seq_start = (i % n_pos) == 0
    u = u_ref[...]
    halo = jnp.where(seq_start, 0.0, halo_ref[...])
    ue = jnp.concatenate([halo, u], axis=0)
    t = (i % n_pos) * tm + lax.broadcasted_iota(I32, (tm, 1), 0)
    o_ref[:, :A_WIDTH] = ya_ref[...]
    for gi, w in enumerate(POOL_WINDOWS):
        lanes = slice(gi * POOL_GROUP, (gi + 1) * POOL_GROUP)
        s = ue[:, lanes]
        span = 1
        while span < w:
            s = s + pltpu.roll(s, span, 0)
            span *= 2
        count = jnp.minimum(t + 1, w).astype(F32)
        pooled = s[POOL_HALO:, :] / count - u[:, lanes]
        y = _dot(pooled.astype(MXU_DTYPE), pw_ref[gi]) * ps_ref[:, lanes]
        o_ref[:, A_WIDTH + gi * POOL_GROUP: A_WIDTH + (gi + 1) * POOL_GROUP] = y.astype(o_ref.dtype)


def _pool(ya, u, pool_w, pool_scale, L, tm):
    T = u.shape[0]
    n_pos = L // tm
    row = lambda i: (i, 0)
    halo_map = lambda i: (jnp.maximum(i * (tm // POOL_HALO) - 1, 0), 0)
    return pl.pallas_call(
        functools.partial(_pool_kernel, n_pos=n_pos),
        grid=(T // tm,),
        in_specs=[pl.BlockSpec((tm, A_WIDTH), row), pl.BlockSpec((tm, POOL_WIDTH), row),
                  pl.BlockSpec((POOL_HALO, POOL_WIDTH), halo_map),
                  _resident(pool_w.shape), _resident((1, POOL_WIDTH))],
        out_specs=pl.BlockSpec((tm, A_WIDTH + POOL_WIDTH), row),
        out_shape=jax.ShapeDtypeStruct((T, A_WIDTH + POOL_WIDTH), MXU_DTYPE),
        compiler_params=_params(("parallel",), 32),
        name="pool",
    )(ya, u, u, pool_w, pool_scale)


def _odd_proj_kernel(h_ref, g_ref, w_ref, qkv_ref, z_ref, ba_ref):
    xn = _rms(h_ref[...], g_ref[...]).astype(MXU_DTYPE)
    n_qkv = qkv_ref.shape[1]
    n_z = z_ref.shape[1]
    step = 512
    for c in range(n_qkv // step):
        qkv_ref[:, c * step:(c + 1) * step] = _dot(xn, w_ref[:, c * step:(c + 1) * step])
    for c in range(n_z // step):
        z_ref[:, c * step:(c + 1) * step] = _dot(xn, w_ref[:, n_qkv + c * step: n_qkv + (c + 1) * step])
    ba_ref[...] = _dot(xn, w_ref[:, n_qkv + n_z:])


def _odd_proj(h2, g, w_all, tm):
    T, D = h2.shape
    N = w_all.shape[1]
    row = lambda i: (i, 0)
    return pl.pallas_call(
        _odd_proj_kernel,
        grid=(T // tm,),
        in_specs=[pl.BlockSpec((tm, D), row), _resident((1, D)), _resident((D, N))],
        out_specs=[pl.BlockSpec((tm, 3 * C_WIDTH), row), pl.BlockSpec((tm, C_WIDTH), row), pl.BlockSpec((tm, LANES), row)],
        out_shape=[jax.ShapeDtypeStruct((T, 3 * C_WIDTH), F32), jax.ShapeDtypeStruct((T, C_WIDTH), F32),
                   jax.ShapeDtypeStruct((T, LANES), F32)],
        compiler_params=_params(("parallel",), 48),
        name="odd_proj",
    )(h2, g, w_all)


def _gates_kernel(ba_ref, alog_ref, dtb_ref, gate_ref, cum_ref):
    tm = ba_ref.shape[0]
    x = ba_ref[...]
    lane = lax.broadcasted_iota(I32, x.shape, 1)
    beta = jax.nn.sigmoid(x)
    g = -jnp.exp(alog_ref[...]) * jnp.logaddexp(x + dtb_ref[...], 0.0)
    gate = jnp.where(lane < C_HEADS, beta, jnp.where(lane < 2 * C_HEADS, g, 0.0))
    gate_ref[...] = gate
    ii = lax.broadcasted_iota(I32, (CHUNK, CHUNK), 0)
    jj = lax.broadcasted_iota(I32, (CHUNK, CHUNK), 1)
    tri = (ii >= jj).astype(F32)
    for c in range(tm // CHUNK):
        rows = slice(c * CHUNK, (c + 1) * CHUNK)
        cum_ref[rows, :] = _dot3(tri, gate[rows, :])


def _gates(ba, alog_row, dtb_row, tm):
    T = ba.shape[0]
    row = lambda i: (i, 0)
    return pl.pallas_call(
        _gates_kernel,
        grid=(T // tm,),
        in_specs=[pl.BlockSpec((tm, LANES), row), _resident((1, LANES)), _resident((1, LANES))],
        out_specs=[pl.BlockSpec((tm, LANES), row), pl.BlockSpec((tm, LANES), row)],
        out_shape=[jax.ShapeDtypeStruct((T, LANES), F32), jax.ShapeDtypeStruct((T, LANES), F32)],
        compiler_params=_params(("parallel",), 16),
        name="gates",
    )(ba, alog_row, dtb_row)


CONV_HALO = 8


def _gdn_kernel(xq_ref, xk_ref, xv_ref, hq_ref, hk_ref, hv_ref, wq_ref, wk_ref, wv_ref,
                gcol_ref, grow_ref, z_ref, on_ref, y_ref, state_ref):
    rows = xq_ref.shape[0]
    seq_start = pl.program_id(2) == 0

    @pl.when(seq_start)
    def _():
        state_ref[...] = jnp.zeros(state_ref.shape, F32)

    def conv_silu(x_ref, halo_ref, w_ref):
        x = x_ref[...]
        halo = jnp.where(seq_start, 0.0, halo_ref[...])
        xe = jnp.concatenate([halo, x], axis=0)
        w = w_ref[...]
        y = w[CONV_WIDTH - 1:CONV_WIDTH, :] * x
        for j in range(CONV_WIDTH - 1):
            back = CONV_WIDTH - 1 - j
            y = y + w[j:j + 1, :] * pltpu.roll(xe, back, 0)[CONV_HALO:, :]
        return _silu(y)

    def l2n(x):
        return x * lax.rsqrt(jnp.sum(x * x, axis=-1, keepdims=True) + NORM_EPS)

    q = l2n(conv_silu(xq_ref, hq_ref, wq_ref)) * (C_HEAD_DIM ** -0.5)
    k = l2n(conv_silu(xk_ref, hk_ref, wk_ref))
    v = conv_silu(xv_ref, hv_ref, wv_ref)
    gcol = gcol_ref[0]
    grow = grow_ref[0, 0]
    z = z_ref[...]

    ii = lax.broadcasted_iota(I32, (CHUNK, CHUNK), 0)
    jj = lax.broadcasted_iota(I32, (CHUNK, CHUNK), 1)
    tril = ii >= jj
    strict = ii > jj
    eye = (ii == jj).astype(F32)

    for c in range(rows // CHUNK):
        sl = slice(c * CHUNK, (c + 1) * CHUNK)
        qc, kc, vc = q[sl], k[sl], v[sl]
        beta = gcol[sl, 0:1]
        gc = gcol[sl, 1:2]
        gc_row = grow[1:2, c * CHUNK:(c + 1) * CHUNK]
        g_last = gc[CHUNK - 1:CHUNK, :]
        decay = jnp.exp(jnp.where(tril, gc - gc_row, -jnp.inf))
        kb = kc.astype(MXU_DTYPE)
        lmat = jnp.where(strict, beta * _dot_nt(kb, kb) * decay, 0.0)
        a = -lmat
        tmat = eye + a
        span = 2
        while span < CHUNK:
            a = _dot3(a, a)
            tmat = tmat + _dot3(tmat, a)
            span *= 2
        rhs = jnp.concatenate([vc * beta, kc * (beta * jnp.exp(gc))], axis=1).astype(MXU_DTYPE)
        solved = _dot(tmat.astype(MXU_DTYPE), rhs)
        value, k_cumdecay = solved[:, :C_HEAD_DIM], solved[:, C_HEAD_DIM:]
        attn = _dot_nt(qc.astype(MXU_DTYPE), kb) * decay
        q_dec = (qc * jnp.exp(gc)).astype(MXU_DTYPE)
        k_dec = (kc * jnp.exp(g_last - gc)).astype(MXU_DTYPE)

        state = state_ref[...]
        both = _dot(jnp.concatenate([k_cumdecay.astype(MXU_DTYPE), q_dec], axis=0), state.astype(MXU_DTYPE))
        v_new = value - both[:CHUNK]
        v_new_b = v_new.astype(MXU_DTYPE)
        o = both[CHUNK:] + _dot(attn.astype(MXU_DTYPE), v_new_b)
        state_ref[...] = state * jnp.exp(g_last) + lax.dot_general(k_dec, v_new_b, TN_DIMS, preferred_element_type=F32)

        y = _rms(o, on_ref[...]) * _silu(z[sl])
        y_ref[sl, :] = y.astype(y_ref.dtype)


def _gdn(qkv, conv_w, gcol, grow, z, o_norm, B, L, rows):
    T = B * L
    H = C_HEADS
    nr = L // rows
    hb = rows // CONV_HALO

    def x_spec(part):
        return pl.BlockSpec((rows, C_HEAD_DIM), lambda b, h, r: (b * nr + r, part * H + h))

    def halo_spec(part):
        return pl.BlockSpec((CONV_HALO, C_HEAD_DIM), lambda b, h, r: (jnp.maximum((b * nr + r) * hb - 1, 0), part * H + h))

    def w_spec(part):
        return pl.BlockSpec((CONV_WIDTH, C_HEAD_DIM), lambda b, h, r: (0, part * H + h))

    return pl.pallas_call(
        _gdn_kernel,
        grid=(B, H, nr),
        in_specs=[x_spec(0), x_spec(1), x_spec(2), halo_spec(0), halo_spec(1), halo_spec(2),
                  w_spec(0), w_spec(1), w_spec(2),
                  pl.BlockSpec((1, rows, 2), lambda b, h, r: (h, b * nr + r, 0)),
                  pl.BlockSpec((1, 1, 2, rows), lambda b, h, r: (b, h, 0, r)),
                  pl.BlockSpec((rows, C_HEAD_DIM), lambda b, h, r: (b * nr + r, h)),
                  pl.BlockSpec((1, C_HEAD_DIM), lambda b, h, r: (0, 0))],
        out_specs=pl.BlockSpec((rows, C_HEAD_DIM), lambda b, h, r: (b * nr + r, h)),
        out_shape=jax.ShapeDtypeStruct((T, C_WIDTH), MXU_DTYPE),
        scratch_shapes=[pltpu.VMEM((C_HEAD_DIM, C_HEAD_DIM), F32)],
        compiler_params=_params(("parallel", "parallel", "arbitrary"), 32),
        name="gdn",
    )(qkv, qkv, qkv, qkv, qkv, qkv, conv_w, conv_w, conv_w, gcol, grow, z, o_norm)


def _out_mlp_kernel(h_ref, y_ref, wo_ref, g_ref, w1_ref, w2_ref, fg_ref, o_ref, *, ff_chunk, final):
    h1 = h_ref[...] + _dot(y_ref[...], wo_ref[...])
    xn = _rms(h1, g_ref[...]).astype(MXU_DTYPE)
    acc = h1
    for c in range(w1_ref.shape[1] // ff_chunk):
        cols = slice(c * ff_chunk, (c + 1) * ff_chunk)
        a = jnp.square(jnp.maximum(_dot(xn, w1_ref[:, cols]), 0.0)).astype(MXU_DTYPE)
        acc = acc + _dot(a, w2_ref[cols, :])
    if final:
        acc = _rms(acc, fg_ref[...])
    o_ref[...] = acc


def _out_mlp(h2, y, w_out, g, w1, w2, final_g, final, tm):
    T, D = h2.shape
    dff = w1.shape[1]
    row = lambda i: (i, 0)
    return pl.pallas_call(
        functools.partial(_out_mlp_kernel, ff_chunk=512, final=final),
        grid=(T // tm,),
        in_specs=[pl.BlockSpec((tm, D), row), pl.BlockSpec((tm, y.shape[1]), row), _resident(w_out.shape),
                  _resident((1, D)), _resident((D, dff)), _resident((dff, D)), _resident((1, D))],
        out_specs=pl.BlockSpec((tm, D), row),
        out_shape=jax.ShapeDtypeStruct((T, D), F32),
        compiler_params=_params(("parallel",), 48),
        name="out_mlp",
    )(h2, y, w_out, g, w1, w2, final_g)


def _rope_tables(L):
    inv = ROPE_THETA ** (-jnp.arange(0, A_HEAD_DIM, 2, dtype=F32) / A_HEAD_DIM)
    ang = jnp.arange(L, dtype=F32)[:, None] * inv[None, :]
    cos, sin = jnp.cos(ang), jnp.sin(ang)
    one, zero = jnp.ones((L, 64), F32), jnp.zeros((L, 64), F32)
    cos2 = jnp.concatenate([cos, cos, cos, cos], axis=1)
    sin2 = jnp.concatenate([-sin, sin, -sin, sin], axis=1)
    cosk = jnp.concatenate([cos, cos, one], axis=1)
    sink = jnp.concatenate([-sin, sin, zero], axis=1)
    return cos2, sin2, cosk, sink


def _pick_tile(L, want):
    t = min(want, L)
    while L % t:
        t //= 2
    return t


def kernel(x, mix_norm, mlp_norm, w_ff1, w_ff2, ev_w_in, ev_kv_norm, ev_w_uk, ev_w_uv, ev_pool_w, ev_pool_scale, ev_w_out, od_w_in, od_conv_w, od_a_log, od_dt_bias, od_o_norm, od_w_out, final_norm):
    B, L, D = x.shape
    T = B * L
    depth = mix_norm.shape[0]
    mm = MXU_DTYPE
    h = x.reshape(T, D)
    tabs = _rope_tables(L)
    tm = _pick_tile(L, 256)
    tm_mlp = _pick_tile(L, 512)
    tq = _pick_tile(L, 128)
    kc = _pick_tile(L, 256)
    gdn_rows = _pick_tile(L, 256)
    fg = final_norm.reshape(1, D)

    for layer in range(depth):
        j = layer // 2
        g_mix = mix_norm[layer].reshape(1, D)
        if layer % 2 == 0:
            w = ev_w_in[j]
            c0 = A_WIDTH
            c1 = c0 + A_KV_RANK
            c2 = c1 + IDX_HEADS * IDX_DIM
            c3 = c2 + IDX_DIM
            c4 = c3 + IDX_HEADS
            pad = jnp.zeros((D, LANES - IDX_DIM - IDX_HEADS), w.dtype)
            w_all = jnp.concatenate([w[:, :c0], w[:, c1:c2], w[:, c4:], w[:, c0:c1], w[:, c2:c3], w[:, c3:c4], pad], axis=1).astype(mm)
            wkv = jnp.concatenate([ev_w_uk[j], ev_w_uv[j]], axis=1).astype(mm)
            qt, qit, u, kv, vt, kib, wit = _even_proj(h, g_mix, w_all, ev_kv_norm[j].reshape(1, -1), wkv, tabs, L, tm, tq, kc)
            ya = _dsa(qt, qit, wit, kv, vt, kib, B, L, tq, kc)
            y = _pool(ya, u, ev_pool_w[j].astype(mm), ev_pool_scale[j].reshape(1, -1), L, tm)
            w_out = ev_w_out[j].astype(mm)
        else:
            w = od_w_in[j]
            n_main = 4 * C_WIDTH
            pad = jnp.zeros((D, LANES - 2 * C_HEADS), w.dtype)
            w_all = jnp.concatenate([w, pad], axis=1).astype(mm)
            qkv, z, ba = _odd_proj(h, g_mix, w_all, tm)
            zero8 = jnp.zeros((C_HEADS,), F32)
            lane_pad = jnp.zeros((LANES - 2 * C_HEADS,), F32)
            alog_row = jnp.concatenate([zero8, od_a_log[j], lane_pad]).reshape(1, LANES)
            dtb_row = jnp.concatenate([zero8, od_dt_bias[j], lane_pad]).reshape(1, LANES)
            gate, cum = _gates(ba, alog_row, dtb_row, tm)
            beta = gate[:, :C_HEADS]
            gcum = cum[:, C_HEADS:2 * C_HEADS]
            gcol = jnp.stack([beta.T, gcum.T], axis=-1)
            grow = jnp.stack([beta.reshape(B, L, C_HEADS), gcum.reshape(B, L, C_HEADS)], axis=1)
            grow = jnp.transpose(grow, (0, 3, 1, 2))
            y = _gdn(qkv, od_conv_w[j], gcol, grow, z, od_o_norm[j].reshape(1, -1), B, L, gdn_rows)
            w_out = od_w_out[j].astype(mm)
        h = _out_mlp(h, y, w_out, mlp_norm[layer].reshape(1, D), w_ff1[layer].astype(mm), w_ff2[layer].astype(mm),
                     fg, layer == depth - 1, tm_mlp)
    return h.reshape(B, L, D)
```

```python
import functools

import numpy as np
import jax
import jax.numpy as jnp
from jax import lax
from jax.experimental import pallas as pl
from jax.experimental.pallas import tpu as pltpu

F32 = jnp.float32
I32 = jnp.int32
MXU_DTYPE = jnp.bfloat16

A_HEADS = 8
A_HEAD_DIM = 64
A_WIDTH = A_HEADS * A_HEAD_DIM
A_KV_RANK = 128
IDX_HEADS = 8
IDX_DIM = 64
TOPK_MAX = 256
POOL_WINDOWS = (2, 4, 8, 16)
POOL_GROUP = 128
POOL_WIDTH = POOL_GROUP * len(POOL_WINDOWS)
C_HEADS = 8
C_HEAD_DIM = 128
C_WIDTH = C_HEADS * C_HEAD_DIM
CONV_WIDTH = 4
CHUNK = 128
ROPE_THETA = 10000.0
NORM_EPS = 1e-6

LANES = 128
INT_MIN = np.int32(-2 ** 31)
MASKED_LOGIT = -1e30
COUNT_ROWS = 32
LOG2_E = float(np.log2(np.e))

NT_DIMS = (((1,), (1,)), ((), ()))


def _params(semantics, vmem_mb):
    return pltpu.CompilerParams(dimension_semantics=semantics, vmem_limit_bytes=vmem_mb * 1024 * 1024)


def _dot(a, b):
    return jnp.dot(a, b, preferred_element_type=F32)


def _dot_nt(a, b):
    return lax.dot_general(a, b, NT_DIMS, preferred_element_type=F32)


def _split(a):
    hi = a.astype(MXU_DTYPE)
    lo = (a - hi.astype(F32)).astype(MXU_DTYPE)
    return hi, lo


def _dot3(a, b):
    ah, al = _split(a)
    bh, bl = _split(b)
    return _dot(ah, bh) + (_dot(ah, bl) + _dot(al, bh))


def _rms(x, g):
    return x * lax.rsqrt(jnp.mean(x * x, axis=-1, keepdims=True) + NORM_EPS) * g


def _silu(x):
    return x * jax.nn.sigmoid(x)


def _resident(shape):
    nd = len(shape)
    return pl.BlockSpec(shape, lambda *_: (0,) * nd, pipeline_mode=pl.Buffered(1))


def _rope(x, cos, sin_signed, first_half):
    partner = jnp.where(first_half, pltpu.roll(x, LANES - 32, 1), pltpu.roll(x, 32, 1))
    return x * cos + partner * sin_signed


def _even_proj_kernel(h_ref, g_ref, w_ref, kvn_ref, wkv_ref, cos_ref, sin_ref, cosk_ref, sink_ref,
                      qt_ref, qit_ref, u_ref, kv_ref, vt_ref, kib_ref, wit_ref, *, tq, kc):
    tm = h_ref.shape[0]
    xn = _rms(h_ref[...], g_ref[...]).astype(MXU_DTYPE)
    p = _dot(xn, w_ref[...])
    lane = lax.broadcasted_iota(I32, (tm, LANES), 1)
    first_half = (lane % 64) < 32
    cos, sin = cos_ref[...], sin_ref[...]
    cosk, sink = cosk_ref[...], sink_ref[...]

    def heads_out(o_ref, base, scale):
        for s in range(4):
            r = _rope(p[:, base + s * LANES: base + (s + 1) * LANES], cos, sin, first_half)
            if scale != 1.0:
                r = r * scale
            for j in range(tm // tq):
                rt = r[j * tq:(j + 1) * tq, :].T.astype(o_ref.dtype)
                o_ref[j, :, 2 * s * tq:(2 * s + 1) * tq] = rt[:64, :]
                o_ref[j, :, (2 * s + 1) * tq:(2 * s + 2) * tq] = rt[64:, :]

    heads_out(qt_ref, 0, (A_HEAD_DIM ** -0.5) * LOG2_E)
    heads_out(qit_ref, A_WIDTH, 1.0)
    u_ref[...] = p[:, 1024:1536]
    ckv = _rms(p[:, 1536:1664], kvn_ref[...]).astype(MXU_DTYPE)
    kv = _rope(_dot(ckv, wkv_ref[...]), cosk, sink, first_half)
    kv_ref[...] = kv.astype(kv_ref.dtype)
    for j in range(tm // kc):
        vt_ref[j] = kv[j * kc:(j + 1) * kc, :].T[64:, :].astype(vt_ref.dtype)
    kiw = _rope(p[:, 1664:1792], cosk, sink, first_half)
    kib_ref[...] = kiw.astype(kib_ref.dtype)
    wit = kiw.T[64:64 + IDX_HEADS, :]
    wit_ref[...] = (wit * (IDX_HEADS ** -0.5)) * (IDX_DIM ** -0.5)


def _even_proj(h2, g, w_all, kvn, wkv, tabs, L, tm, tq, kc):
    T, D = h2.shape
    n_pos = L // tm
    N = w_all.shape[1]
    row = lambda i: (i, 0)
    pos = lambda i: (i % n_pos, 0)
    tab_spec = pl.BlockSpec((tm, LANES), pos)
    head_spec = pl.BlockSpec((tm // tq, 64, A_HEADS * tq), lambda i: (i, 0, 0))
    head_shape = jax.ShapeDtypeStruct((T // tq, 64, A_HEADS * tq), MXU_DTYPE)
    return pl.pallas_call(
        functools.partial(_even_proj_kernel, tq=tq, kc=kc),
        grid=(T // tm,),
        in_specs=[pl.BlockSpec((tm, D), row), _resident((1, D)), _resident((D, N)), _resident((1, A_KV_RANK)),
                  _resident((A_KV_RANK, LANES)), tab_spec, tab_spec, tab_spec, tab_spec],
        out_specs=[head_spec, head_spec, pl.BlockSpec((tm, POOL_WIDTH), row), pl.BlockSpec((tm, LANES), row),
                   pl.BlockSpec((tm // kc, 64, kc), lambda i: (i, 0, 0)), pl.BlockSpec((tm, LANES), row),
                   pl.BlockSpec((IDX_HEADS, tm), lambda i: (0, i))],
        out_shape=[head_shape, head_shape, jax.ShapeDtypeStruct((T, POOL_WIDTH), F32),
                   jax.ShapeDtypeStruct((T, LANES), MXU_DTYPE), jax.ShapeDtypeStruct((T // kc, 64, kc), MXU_DTYPE),
                   jax.ShapeDtypeStruct((T, LANES), MXU_DTYPE), jax.ShapeDtypeStruct((IDX_HEADS, T), F32)],
        compiler_params=_params(("parallel",), 40),
        name="even_proj",
    )(h2, g, w_all, kvn, wkv, *tabs)


def _dsa_kernel(qt_ref, qit_ref, wit_ref, kv_ref, vt_ref, kib_ref, o_ref,
                keys_ref, s_ref, p_ref, cap_ref, m_ref, l_ref, acc_ref, *, tq, kc, topk, pos_bits):
    i = pl.program_id(1)
    n_chunks = ((i + 1) * tq + kc - 1) // kc
    nh = A_HEADS
    qpos = i * tq + lax.broadcasted_iota(I32, (1, tq), 1)
    kpos0 = lax.broadcasted_iota(I32, (kc, tq), 0)
    wit = wit_ref[...]

    def score_chunk(c, carry):
        kic = kib_ref[pl.ds(pl.multiple_of(c * kc, kc), kc), :]
        s_ref[...] = _dot(kic[:, :IDX_DIM], qit_ref[0])
        score = jnp.zeros((kc, tq), F32)
        for h in range(nh):
            score = score + jnp.maximum(s_ref[:, h * tq:(h + 1) * tq], 0.0) * wit[h:h + 1, :]
        bits = lax.bitcast_convert_type(score, I32)
        mag = bits & np.int32(0x7FFFFFFF)
        key = jnp.where(bits < 0, -mag, mag)
        keys_ref[c] = jnp.where(c * kc + kpos0 <= qpos, key, INT_MIN)
        return carry

    lax.fori_loop(0, n_chunks, score_chunk, 0)

    def count(pred):
        def body(c, acc):
            w = jnp.where(pred(keys_ref[c], c), 1, 0)
            return acc + jnp.sum(w.reshape(kc // COUNT_ROWS, COUNT_ROWS, tq), axis=0)
        acc = lax.fori_loop(0, n_chunks, body, jnp.zeros((COUNT_ROWS, tq), I32))
        return jnp.sum(acc, axis=0, keepdims=True)

    k_eff = jnp.minimum(qpos + 1, topk)

    def thr_bit(it, thr):
        cand = thr ^ lax.shift_left(np.int32(1), 31 - it)
        cnt = count(lambda kb, c: kb >= cand)
        return jnp.where(cnt >= k_eff, cand, thr)

    thr = lax.fori_loop(0, 32, thr_bit, jnp.full((1, tq), INT_MIN, I32))

    need = k_eff - count(lambda kb, c: kb > thr)
    n_ge = count(lambda kb, c: kb >= thr)
    has_tie = jnp.max(n_ge - k_eff) > 0

    def cut_bit(it, cut):
        cand = cut | lax.shift_left(np.int32(1), pos_bits - 1 - it)
        cnt = count(lambda kb, c: jnp.logical_and(kb == thr, c * kc + kpos0 < cand))
        return jnp.where(cnt < need, cand, cut)

    cut = lax.cond(has_tie,
                   lambda: lax.fori_loop(0, pos_bits, cut_bit, jnp.zeros((1, tq), I32)),
                   lambda: jnp.full((1, tq), 2 ** pos_bits - 1, I32))

    m_ref[...] = jnp.full(m_ref.shape, MASKED_LOGIT, F32)
    l_ref[...] = jnp.zeros(l_ref.shape, F32)
    acc_ref[...] = jnp.zeros(acc_ref.shape, F32)

    def attend_chunk(c, carry):
        kvc = kv_ref[pl.ds(pl.multiple_of(c * kc, kc), kc), :]
        kb = keys_ref[c]
        sel = jnp.logical_or(kb > thr, jnp.logical_and(kb == thr, c * kc + kpos0 <= cut))
        cap_ref[...] = jnp.where(sel, jnp.inf, MASKED_LOGIT)
        s_ref[...] = _dot(kvc[:, :A_HEAD_DIM], qt_ref[0])
        for h in range(nh):
            cols = slice(h * tq, (h + 1) * tq)
            s = jnp.minimum(s_ref[:, cols], cap_ref[...])
            m_old = m_ref[:, cols]
            m_new = jnp.maximum(m_old, jnp.max(s, axis=0, keepdims=True))
            alpha = jnp.exp2(m_old - m_new)
            p = jnp.exp2(s - m_new)
            l_ref[:, cols] = alpha * l_ref[:, cols] + jnp.sum(p, axis=0, keepdims=True)
            m_ref[:, cols] = m_new
            p_ref[:, cols] = p.astype(p_ref.dtype)
            acc_ref[:, cols] = acc_ref[:, cols] * alpha
        acc_ref[...] += _dot(vt_ref[c], p_ref[...])
        return carry

    lax.fori_loop(0, n_chunks, attend_chunk, 0)

    out_t = jnp.concatenate([acc_ref[:, h * tq:(h + 1) * tq] / l_ref[:, h * tq:(h + 1) * tq] for h in range(nh)], axis=0)
    o_ref[...] = out_t.T.astype(o_ref.dtype)


def _dsa(qt, qit, wit, kv, vt, kib, B, L, tq, kc):
    T = B * L
    nq = L // tq
    nkc = L // kc
    topk = min(TOPK_MAX, L // 4)
    pos_bits = max(1, int(np.ceil(np.log2(L))))
    head_spec = pl.BlockSpec((1, 64, A_HEADS * tq), lambda b, i: (b * nq + i, 0, 0))
    seq_spec = pl.BlockSpec((L, LANES), lambda b, i: (b, 0))
    kern = functools.partial(_dsa_kernel, tq=tq, kc=kc, topk=topk, pos_bits=pos_bits)
    return pl.pallas_call(
        kern,
        grid=(B, nq),
        in_specs=[head_spec, head_spec, pl.BlockSpec((IDX_HEADS, tq), lambda b, i: (0, b * nq + i)),
                  seq_spec, pl.BlockSpec((nkc, 64, kc), lambda b, i: (b, 0, 0)), seq_spec],
        out_specs=pl.BlockSpec((tq, A_WIDTH), lambda b, i: (b * nq + i, 0)),
        out_shape=jax.ShapeDtypeStruct((T, A_WIDTH), MXU_DTYPE),
        scratch_shapes=[pltpu.VMEM((nkc, kc, tq), I32), pltpu.VMEM((kc, A_HEADS * tq), F32),
                        pltpu.VMEM((kc, A_HEADS * tq), MXU_DTYPE), pltpu.VMEM((kc, tq), F32),
                        pltpu.VMEM((1, A_HEADS * tq), F32), pltpu.VMEM((1, A_HEADS * tq), F32),
                        pltpu.VMEM((A_HEAD_DIM, A_HEADS * tq), F32)],
        compiler_params=_params(("parallel", "arbitrary"), 40),
        name="dsa",
    )(qt, qit, wit, kv, vt, kib)


POOL_HALO = 16


def _pool_kernel(ya_ref, u_ref, halo_ref, pw_ref, ps_ref, o_ref, *, n_pos):
    tm = u_ref.shape[0]
    i = pl.program_id(0)
    seq_start = (i % n_pos) == 0
    u = u_ref[...]
    halo = jnp.where(seq_start, 0.0, halo_ref[...])
    ue = jnp.concatenate([halo, u], axis=0)
    t = (i % n_pos) * tm + lax.broadcasted_iota(I32, (tm, 1), 0)
    o_ref[:, :A_WIDTH] = ya_ref[...]
    for gi, w in enumerate(POOL_WINDOWS):
        lanes = slice(gi * POOL_GROUP, (gi + 1) * POOL_GROUP)
        s = ue[:, lanes]
        span = 1
        while span < w:
            s = s + pltpu.roll(s, span, 0)
            span *= 2
        count = jnp.minimum(t + 1, w).astype(F32)
        pooled = s[POOL_HALO:, :] / count - u[:, lanes]
        y = _dot(pooled.astype(MXU_DTYPE), pw_ref[gi]) * ps_ref[:, lanes]
        o_ref[:, A_WIDTH + gi * POOL_GROUP: A_WIDTH + (gi + 1) * POOL_GROUP] = y.astype(o_ref.dtype)


def _pool(ya, u, pool_w, pool_scale, L, tm):
    T = u.shape[0]
    n_pos = L // tm
    row = lambda i: (i, 0)
    halo_map = lambda i: (jnp.maximum(i * (tm // POOL_HALO) - 1, 0), 0)
    return pl.pallas_call(
        functools.partial(_pool_kernel, n_pos=n_pos),
        grid=(T // tm,),
        in_specs=[pl.BlockSpec((tm, A_WIDTH), row), pl.BlockSpec((tm, POOL_WIDTH), row),
                  pl.BlockSpec((POOL_HALO, POOL_WIDTH), halo_map),
                  _resident(pool_w.shape), _resident((1, POOL_WIDTH))],
        out_specs=pl.BlockSpec((tm, A_WIDTH + POOL_WIDTH), row),
        out_shape=jax.ShapeDtypeStruct((T, A_WIDTH + POOL_WIDTH), MXU_DTYPE),
        compiler_params=_params(("parallel",), 32),
        name="pool",
    )(ya, u, u, pool_w, pool_scale)


def _odd_proj_kernel(h_ref, g_ref, w_ref, qkv_ref, z_ref, ba_ref):
    xn = _rms(h_ref[...], g_ref[...]).astype(MXU_DTYPE)
    n_qkv = qkv_ref.shape[1]
    n_z = z_ref.shape[1]
    step = 512
    for c in range(n_qkv // step):
        qkv_ref[:, c * step:(c + 1) * step] = _dot(xn, w_ref[:, c * step:(c + 1) * step])
    for c in range(n_z // step):
        z_ref[:, c * step:(c + 1) * step] = _dot(xn, w_ref[:, n_qkv + c * step: n_qkv + (c + 1) * step])
    ba_ref[...] = _dot(xn, w_ref[:, n_qkv + n_z:])


def _odd_proj(h2, g, w_all, tm):
    T, D = h2.shape
    N = w_all.shape[1]
    row = lambda i: (i, 0)
    return pl.pallas_call(
        _odd_proj_kernel,
        grid=(T // tm,),
        in_specs=[pl.BlockSpec((tm, D), row), _resident((1, D)), _resident((D, N))],
        out_specs=[pl.BlockSpec((tm, 3 * C_WIDTH), row), pl.BlockSpec((tm, C_WIDTH), row), pl.BlockSpec((tm, LANES), row)],
        out_shape=[jax.ShapeDtypeStruct((T, 3 * C_WIDTH), F32), jax.ShapeDtypeStruct((T, C_WIDTH), F32),
                   jax.ShapeDtypeStruct((T, LANES), F32)],
        compiler_params=_params(("parallel",), 48),
        name="odd_proj",
    )(h2, g, w_all)


def _gates_kernel(ba_ref, alog_ref, dtb_ref, gate_ref, cum_ref):
    tm = ba_ref.shape[0]
    x = ba_ref[...]
    lane = lax.broadcasted_iota(I32, x.shape, 1)
    beta = jax.nn.sigmoid(x)
    g = -jnp.exp(alog_ref[...]) * jnp.logaddexp(x + dtb_ref[...], 0.0)
    gate = jnp.where(lane < C_HEADS, beta, jnp.where(lane < 2 * C_HEADS, g, 0.0))
    gate_ref[...] = gate
    ii = lax.broadcasted_iota(I32, (CHUNK, CHUNK), 0)
    jj = lax.broadcasted_iota(I32, (CHUNK, CHUNK), 1)
    tri = (ii >= jj).astype(F32)
    for c in range(tm // CHUNK):
        rows = slice(c * CHUNK, (c + 1) * CHUNK)
        cum_ref[rows, :] = _dot3(tri, gate[rows, :])


def _gates(ba, alog_row, dtb_row, tm):
    T = ba.shape[0]
    row = lambda i: (i, 0)
    return pl.pallas_call(
        _gates_kernel,
        grid=(T // tm,),
        in_specs=[pl.BlockSpec((tm, LANES), row), _resident((1, LANES)), _resident((1, LANES))],
        out_specs=[pl.BlockSpec((tm, LANES), row), pl.BlockSpec((tm, LANES), row)],
        out_shape=[jax.ShapeDtypeStruct((T, LANES), F32), jax.ShapeDtypeStruct((T, LANES), F32)],
        compiler_params=_params(("parallel",), 16),
        name="gates",
    )(ba, alog_row, dtb_row)


CONV_HALO = 8


def _gdn_prep_kernel(xq_ref, xk_ref, xv_ref, hq_ref, hk_ref, hv_ref, wq_ref, wk_ref, wv_ref, gcol_ref, grow_ref,
                     kcd_ref, qd_ref, kdt_ref, attn_ref, val_ref):
    rows = xq_ref.shape[0]
    seq_start = pl.program_id(2) == 0

    def conv_silu(x_ref, halo_ref, w_ref):
        x = x_ref[...]
        halo = jnp.where(seq_start, 0.0, halo_ref[...])
        xe = jnp.concatenate([halo, x], axis=0)
        w = w_ref[...]
        y = w[CONV_WIDTH - 1:CONV_WIDTH, :] * x
        for j in range(CONV_WIDTH - 1):
            back = CONV_WIDTH - 1 - j
            y = y + w[j:j + 1, :] * pltpu.roll(xe, back, 0)[CONV_HALO:, :]
        return _silu(y)

    def l2n(x):
        return x * lax.rsqrt(jnp.sum(x * x, axis=-1, keepdims=True) + NORM_EPS)

    q = l2n(conv_silu(xq_ref, hq_ref, wq_ref)) * (C_HEAD_DIM ** -0.5)
    k = l2n(conv_silu(xk_ref, hk_ref, wk_ref))
    v = conv_silu(xv_ref, hv_ref, wv_ref)
    gcol = gcol_ref[0]
    grow = grow_ref[0, 0]

    ii = lax.broadcasted_iota(I32, (CHUNK, CHUNK), 0)
    jj = lax.broadcasted_iota(I32, (CHUNK, CHUNK), 1)
    tril = ii >= jj
    strict = ii > jj
    eye = (ii == jj).astype(F32)

    sls = [slice(c * CHUNK, (c + 1) * CHUNK) for c in range(rows // CHUNK)]
    beta = [gcol[s, 0:1] for s in sls]
    gc = [gcol[s, 1:2] for s in sls]
    decay = [jnp.exp(jnp.where(tril, g - grow[1:2, s], -jnp.inf)) for g, s in zip(gc, sls)]
    kb = [k[s].astype(MXU_DTYPE) for s in sls]
    kk = [_dot_nt(x, x) for x in kb]
    lmat = [jnp.where(strict, b * x * d, 0.0) for b, x, d in zip(beta, kk, decay)]
    tmat = [eye - jnp.where((ii >> 1) == (jj >> 1), x, 0.0) for x in lmat]
    shift = 1
    while (2 << shift) <= CHUNK:
        couple = jnp.logical_and((ii >> (shift + 1)) == (jj >> (shift + 1)), (ii >> shift) != (jj >> shift))
        cd = [_dot3(jnp.where(couple, x, 0.0), t) for x, t in zip(lmat, tmat)]
        tmat = [t - _dot3(t, y) for t, y in zip(tmat, cd)]
        shift += 1
    eg = [jnp.exp(g) for g in gc]
    for c, s in enumerate(sls):
        rhs = jnp.concatenate([v[s] * beta[c], k[s] * (beta[c] * eg[c])], axis=1).astype(MXU_DTYPE)
        solved = _dot(tmat[c].astype(MXU_DTYPE), rhs)
        val_ref[s, :] = solved[:, :C_HEAD_DIM]
        kcd_ref[s, :] = solved[:, C_HEAD_DIM:].astype(kcd_ref.dtype)
        attn_ref[s, :] = (_dot_nt(q[s].astype(MXU_DTYPE), kb[c]) * decay[c]).astype(attn_ref.dtype)
        qd_ref[s, :] = (q[s] * eg[c]).astype(qd_ref.dtype)
        g_last = gc[c][CHUNK - 1:CHUNK, :]
        kdt_ref[s, :] = (k[s] * jnp.exp(g_last - gc[c])).T.astype(kdt_ref.dtype)


def _gdn_prep(qkv, conv_w, gcol, grow, B, L, rows):
    T = B * L
    H = C_HEADS
    nr = L // rows
    hb = rows // CONV_HALO

    def x_spec(part):
        return pl.BlockSpec((rows, C_HEAD_DIM), lambda b, h, r: (b * nr + r, part * H + h))

    def halo_spec(part):
        return pl.BlockSpec((CONV_HALO, C_HEAD_DIM), lambda b, h, r: (jnp.maximum((b * nr + r) * hb - 1, 0), part * H + h))

    def w_spec(part):
        return pl.BlockSpec((CONV_WIDTH, C_HEAD_DIM), lambda b, h, r: (0, part * H + h))

    out_spec = pl.BlockSpec((rows, C_HEAD_DIM), lambda b, h, r: (b * nr + r, h))
    mm_shape = jax.ShapeDtypeStruct((T, C_WIDTH), MXU_DTYPE)
    return pl.pallas_call(
        _gdn_prep_kernel,
        grid=(B, H, nr),
        in_specs=[x_spec(0), x_spec(1), x_spec(2), halo_spec(0), halo_spec(1), halo_spec(2),
                  w_spec(0), w_spec(1), w_spec(2),
                  pl.BlockSpec((1, rows, 2), lambda b, h, r: (h, b * nr + r, 0)),
                  pl.BlockSpec((1, 1, 2, rows), lambda b, h, r: (b, h, 0, r))],
        out_specs=[out_spec] * 5,
        out_shape=[mm_shape, mm_shape, mm_shape, mm_shape, jax.ShapeDtypeStruct((T, C_WIDTH), F32)],
        compiler_params=_params(("parallel", "parallel", "parallel"), 32),
        name="gdn_prep",
    )(qkv, qkv, qkv, qkv, qkv, qkv, conv_w, conv_w, conv_w, gcol, grow)


def _gdn_scan_kernel(kcd_ref, qd_ref, kdt_ref, attn_ref, val_ref, grow_ref, z_ref, on_ref, y_ref, state_ref):
    rows = kcd_ref.shape[0]
    nh = C_HEADS

    @pl.when(pl.program_id(1) == 0)
    def _():
        state_ref[...] = jnp.zeros(state_ref.shape, F32)

    on = on_ref[...]
    for c in range(rows // CHUNK):
        s = slice(c * CHUNK, (c + 1) * CHUNK)
        cols = [slice(h * C_HEAD_DIM, (h + 1) * C_HEAD_DIM) for h in range(nh)]
        state = [state_ref[h] for h in range(nh)]
        both = [_dot(jnp.concatenate([kcd_ref[s, cl], qd_ref[s, cl]], axis=0), st.astype(MXU_DTYPE))
                for cl, st in zip(cols, state)]
        v_new = [(val_ref[s, cl] - b[:CHUNK]).astype(MXU_DTYPE) for cl, b in zip(cols, both)]
        res = [_dot(jnp.concatenate([attn_ref[s, cl], kdt_ref[s, cl]], axis=0), vn) for cl, vn in zip(cols, v_new)]
        for h in range(nh):
            g_last = grow_ref[0, h, 1:2, (c + 1) * CHUNK - 1:(c + 1) * CHUNK]
            state_ref[h] = state[h] * jnp.exp(g_last) + res[h][CHUNK:]
            o = both[h][CHUNK:] + res[h][:CHUNK]
            y = _rms(o, on) * _silu(z_ref[s, cols[h]])
            y_ref[s, cols[h]] = y.astype(y_ref.dtype)


def _gdn_scan(kcd, qd, kdt, attn, val, grow, z, o_norm, B, L, rows):
    T = B * L
    nr = L // rows
    row = pl.BlockSpec((rows, C_WIDTH), lambda b, r: (b * nr + r, 0))
    return pl.pallas_call(
        _gdn_scan_kernel,
        grid=(B, nr),
        in_specs=[row, row, row, row, row,
                  pl.BlockSpec((1, C_HEADS, 2, rows), lambda b, r: (b, 0, 0, r)),
                  row, pl.BlockSpec((1, C_HEAD_DIM), lambda b, r: (0, 0))],
        out_specs=row,
        out_shape=jax.ShapeDtypeStruct((T, C_WIDTH), MXU_DTYPE),
        scratch_shapes=[pltpu.VMEM((C_HEADS, C_HEAD_DIM, C_HEAD_DIM), F32)],
        compiler_params=_params(("parallel", "arbitrary"), 48),
        name="gdn_scan",
    )(kcd, qd, kdt, attn, val, grow, z, o_norm)


def _out_mlp_kernel(h_ref, y_ref, wo_ref, g_ref, w1_ref, w2_ref, fg_ref, o_ref, *, ff_chunk, final):
    h1 = h_ref[...] + _dot(y_ref[...], wo_ref[...])
    xn = _rms(h1, g_ref[...]).astype(MXU_DTYPE)
    acc = h1
    for c in range(w1_ref.shape[1] // ff_chunk):
        cols = slice(c * ff_chunk, (c + 1) * ff_chunk)
        a = jnp.square(jnp.maximum(_dot(xn, w1_ref[:, cols]), 0.0)).astype(MXU_DTYPE)
        acc = acc + _dot(a, w2_ref[cols, :])
    if final:
        acc = _rms(acc, fg_ref[...])
    o_ref[...] = acc


def _out_mlp(h2, y, w_out, g, w1, w2, final_g, final, tm):
    T, D = h2.shape
    dff = w1.shape[1]
    row = lambda i: (i, 0)
    return pl.pallas_call(
        functools.partial(_out_mlp_kernel, ff_chunk=512, final=final),
        grid=(T // tm,),
        in_specs=[pl.BlockSpec((tm, D), row), pl.BlockSpec((tm, y.shape[1]), row), _resident(w_out.shape),
                  _resident((1, D)), _resident((D, dff)), _resident((dff, D)), _resident((1, D))],
        out_specs=pl.BlockSpec((tm, D), row),
        out_shape=jax.ShapeDtypeStruct((T, D), F32),
        compiler_params=_params(("parallel",), 48),
        name="out_mlp",
    )(h2, y, w_out, g, w1, w2, final_g)


def _rope_tables(L):
    inv = ROPE_THETA ** (-jnp.arange(0, A_HEAD_DIM, 2, dtype=F32) / A_HEAD_DIM)
    ang = jnp.arange(L, dtype=F32)[:, None] * inv[None, :]
    cos, sin = jnp.cos(ang), jnp.sin(ang)
    one, zero = jnp.ones((L, 64), F32), jnp.zeros((L, 64), F32)
    cos2 = jnp.concatenate([cos, cos, cos, cos], axis=1)
    sin2 = jnp.concatenate([-sin, sin, -sin, sin], axis=1)
    cosk = jnp.concatenate([cos, cos, one], axis=1)
    sink = jnp.concatenate([-sin, sin, zero], axis=1)
    return cos2, sin2, cosk, sink


def _pick_tile(L, want):
    t = min(want, L)
    while L % t:
        t //= 2
    return t


def kernel(x, mix_norm, mlp_norm, w_ff1, w_ff2, ev_w_in, ev_kv_norm, ev_w_uk, ev_w_uv, ev_pool_w, ev_pool_scale, ev_w_out, od_w_in, od_conv_w, od_a_log, od_dt_bias, od_o_norm, od_w_out, final_norm):
    B, L, D = x.shape
    T = B * L
    depth = mix_norm.shape[0]
    mm = MXU_DTYPE
    h = x.reshape(T, D)
    tabs = _rope_tables(L)
    tm = _pick_tile(L, 512)
    tm_mlp = _pick_tile(L, 512)
    tq = _pick_tile(L, 128)
    kc = _pick_tile(L, 512)
    gdn_rows = _pick_tile(L, 512)
    fg = final_norm.reshape(1, D)

    for layer in range(depth):
        j = layer // 2
        g_mix = mix_norm[layer].reshape(1, D)
        if layer % 2 == 0:
            w = ev_w_in[j]
            c0 = A_WIDTH
            c1 = c0 + A_KV_RANK
            c2 = c1 + IDX_HEADS * IDX_DIM
            c3 = c2 + IDX_DIM
            c4 = c3 + IDX_HEADS
            pad = jnp.zeros((D, LANES - IDX_DIM - IDX_HEADS), w.dtype)
            w_all = jnp.concatenate([w[:, :c0], w[:, c1:c2], w[:, c4:], w[:, c0:c1], w[:, c2:c3], w[:, c3:c4], pad], axis=1).astype(mm)
            wkv = jnp.concatenate([ev_w_uk[j], ev_w_uv[j]], axis=1).astype(mm)
            qt, qit, u, kv, vt, kib, wit = _even_proj(h, g_mix, w_all, ev_kv_norm[j].reshape(1, -1), wkv, tabs, L, tm, tq, kc)
            ya = _dsa(qt, qit, wit, kv, vt, kib, B, L, tq, kc)
            y = _pool(ya, u, ev_pool_w[j].astype(mm), ev_pool_scale[j].reshape(1, -1), L, tm)
            w_out = ev_w_out[j].astype(mm)
        else:
            w = od_w_in[j]
            pad = jnp.zeros((D, LANES - 2 * C_HEADS), w.dtype)
            w_all = jnp.concatenate([w, pad], axis=1).astype(mm)
            qkv, z, ba = _odd_proj(h, g_mix, w_all, tm)
            zero8 = jnp.zeros((C_HEADS,), F32)
            lane_pad = jnp.zeros((LANES - 2 * C_HEADS,), F32)
            alog_row = jnp.concatenate([zero8, od_a_log[j], lane_pad]).reshape(1, LANES)
            dtb_row = jnp.concatenate([zero8, od_dt_bias[j], lane_pad]).reshape(1, LANES)
            gate, cum = _gates(ba, alog_row, dtb_row, tm)
            beta = gate[:, :C_HEADS]
            gcum = cum[:, C_HEADS:2 * C_HEADS]
            gcol = jnp.stack([beta.T, gcum.T], axis=-1)
            grow = jnp.stack([beta.reshape(B, L, C_HEADS), gcum.reshape(B, L, C_HEADS)], axis=1)
            grow = jnp.transpose(grow, (0, 3, 1, 2))
            kcd, qd, kdt, attn, val = _gdn_prep(qkv, od_conv_w[j], gcol, grow, B, L, gdn_rows)
            y = _gdn_scan(kcd, qd, kdt, attn, val, grow, z, od_o_norm[j].reshape(1, -1), B, L, gdn_rows)
            w_out = od_w_out[j].astype(mm)
        h = _out_mlp(h, y, w_out, mlp_norm[layer].reshape(1, D), w_ff1[layer].astype(mm), w_ff2[layer].astype(mm),
                     fg, layer == depth - 1, tm_mlp)
    return h.reshape(B, L, D)
```

```python
import functools

import numpy as np
import jax
import jax.numpy as jnp
from jax import lax
from jax.experimental import pallas as pl
from jax.experimental.pallas import tpu as pltpu

F32 = jnp.float32
I32 = jnp.int32
MXU_DTYPE = jnp.bfloat16

A_HEADS = 8
A_HEAD_DIM = 64
A_WIDTH = A_HEADS * A_HEAD_DIM
A_KV_RANK = 128
IDX_HEADS = 8
IDX_DIM = 64
TOPK_MAX = 256
POOL_WINDOWS = (2, 4, 8, 16)
POOL_GROUP = 128
POOL_WIDTH = POOL_GROUP * len(POOL_WINDOWS)
C_HEADS = 8
C_HEAD_DIM = 128
C_WIDTH = C_HEADS * C_HEAD_DIM
CONV_WIDTH = 4
CHUNK = 128
ROPE_THETA = 10000.0
NORM_EPS = 1e-6

LANES = 128
INT_MIN = np.int32(-2 ** 31)
MASKED_LOGIT = -1e30
COUNT_ROWS = 32
LOG2_E = float(np.log2(np.e))

NT_DIMS = (((1,), (1,)), ((), ()))


def _params(semantics, vmem_mb):
    return pltpu.CompilerParams(dimension_semantics=semantics, vmem_limit_bytes=vmem_mb * 1024 * 1024)


def _dot(a, b):
    return jnp.dot(a, b, preferred_element_type=F32)


def _dot_nt(a, b):
    return lax.dot_general(a, b, NT_DIMS, preferred_element_type=F32)


def _split(a):
    hi = a.astype(MXU_DTYPE)
    lo = (a - hi.astype(F32)).astype(MXU_DTYPE)
    return hi, lo


def _dot3(a, b):
    ah, al = _split(a)
    bh, bl = _split(b)
    return _dot(ah, bh) + (_dot(ah, bl) + _dot(al, bh))


def _rms(x, g):
    return x * lax.rsqrt(jnp.mean(x * x, axis=-1, keepdims=True) + NORM_EPS) * g


def _silu(x):
    return x * jax.nn.sigmoid(x)


def _resident(shape):
    nd = len(shape)
    return pl.BlockSpec(shape, lambda *_: (0,) * nd, pipeline_mode=pl.Buffered(1))


def _rope(x, cos, sin_signed, first_half):
    partner = jnp.where(first_half, pltpu.roll(x, LANES - 32, 1), pltpu.roll(x, 32, 1))
    return x * cos + partner * sin_signed


def _even_proj_kernel(h_ref, g_ref, w_ref, kvn_ref, wkv_ref, cos_ref, sin_ref, cosk_ref, sink_ref,
                      qt_ref, qit_ref, u_ref, kv_ref, vt_ref, kib_ref, wit_ref, *, tq, kc):
    tm = h_ref.shape[0]
    xn = _rms(h_ref[...], g_ref[...]).astype(MXU_DTYPE)
    p = _dot(xn, w_ref[...])
    lane = lax.broadcasted_iota(I32, (tm, LANES), 1)
    first_half = (lane % 64) < 32
    cos, sin = cos_ref[...], sin_ref[...]
    cosk, sink = cosk_ref[...], sink_ref[...]

    def heads_out(o_ref, base, scale):
        for s in range(4):
            r = _rope(p[:, base + s * LANES: base + (s + 1) * LANES], cos, sin, first_half)
            if scale != 1.0:
                r = r * scale
            for j in range(tm // tq):
                rt = r[j * tq:(j + 1) * tq, :].T.astype(o_ref.dtype)
                o_ref[j, :, 2 * s * tq:(2 * s + 1) * tq] = rt[:64, :]
                o_ref[j, :, (2 * s + 1) * tq:(2 * s + 2) * tq] = rt[64:, :]

    heads_out(qt_ref, 0, (A_HEAD_DIM ** -0.5) * LOG2_E)
    heads_out(qit_ref, A_WIDTH, 1.0)
    u_ref[...] = p[:, 1024:1536]
    ckv = _rms(p[:, 1536:1664], kvn_ref[...]).astype(MXU_DTYPE)
    kv = _rope(_dot(ckv, wkv_ref[...]), cosk, sink, first_half)
    kv_ref[...] = kv.astype(kv_ref.dtype)
    for j in range(tm // kc):
        vt_ref[j] = kv[j * kc:(j + 1) * kc, :].T[64:, :].astype(vt_ref.dtype)
    kiw = _rope(p[:, 1664:1792], cosk, sink, first_half)
    kib_ref[...] = kiw.astype(kib_ref.dtype)
    wit = kiw.T[64:64 + IDX_HEADS, :]
    wit_ref[...] = (wit * (IDX_HEADS ** -0.5)) * (IDX_DIM ** -0.5)


def _even_proj(h2, g, w_all, kvn, wkv, tabs, L, tm, tq, kc):
    T, D = h2.shape
    n_pos = L // tm
    N = w_all.shape[1]
    row = lambda i: (i, 0)
    pos = lambda i: (i % n_pos, 0)
    tab_spec = pl.BlockSpec((tm, LANES), pos)
    head_spec = pl.BlockSpec((tm // tq, 64, A_HEADS * tq), lambda i: (i, 0, 0))
    head_shape = jax.ShapeDtypeStruct((T // tq, 64, A_HEADS * tq), MXU_DTYPE)
    return pl.pallas_call(
        functools.partial(_even_proj_kernel, tq=tq, kc=kc),
        grid=(T // tm,),
        in_specs=[pl.BlockSpec((tm, D), row), _resident((1, D)), _resident((D, N)), _resident((1, A_KV_RANK)),
                  _resident((A_KV_RANK, LANES)), tab_spec, tab_spec, tab_spec, tab_spec],
        out_specs=[head_spec, head_spec, pl.BlockSpec((tm, POOL_WIDTH), row), pl.BlockSpec((tm, LANES), row),
                   pl.BlockSpec((tm // kc, 64, kc), lambda i: (i, 0, 0)), pl.BlockSpec((tm, LANES), row),
                   pl.BlockSpec((IDX_HEADS, tm), lambda i: (0, i))],
        out_shape=[head_shape, head_shape, jax.ShapeDtypeStruct((T, POOL_WIDTH), F32),
                   jax.ShapeDtypeStruct((T, LANES), MXU_DTYPE), jax.ShapeDtypeStruct((T // kc, 64, kc), MXU_DTYPE),
                   jax.ShapeDtypeStruct((T, LANES), MXU_DTYPE), jax.ShapeDtypeStruct((IDX_HEADS, T), F32)],
        compiler_params=_params(("parallel",), 40),
        name="even_proj",
    )(h2, g, w_all, kvn, wkv, *tabs)


def _dsa_kernel(qt_ref, qit_ref, wit_ref, kv_ref, vt_ref, kib_ref, o_ref,
                keys_ref, s_ref, p_ref, cap_ref, m_ref, l_ref, acc_ref, *, tq, kc, topk, pos_bits):
    i = pl.program_id(1)
    n_chunks = ((i + 1) * tq + kc - 1) // kc
    nh = A_HEADS
    qpos = i * tq + lax.broadcasted_iota(I32, (1, tq), 1)
    kpos0 = lax.broadcasted_iota(I32, (kc, tq), 0)
    wit = wit_ref[...]

    def score_chunk(c, carry):
        kic = kib_ref[pl.ds(pl.multiple_of(c * kc, kc), kc), :]
        s_ref[...] = _dot(kic[:, :IDX_DIM], qit_ref[0])
        score = jnp.zeros((kc, tq), F32)
        for h in range(nh):
            score = score + jnp.maximum(s_ref[:, h * tq:(h + 1) * tq], 0.0) * wit[h:h + 1, :]
        bits = lax.bitcast_convert_type(score, I32)
        mag = bits & np.int32(0x7FFFFFFF)
        key = jnp.where(bits < 0, -mag, mag)
        keys_ref[c] = jnp.where(c * kc + kpos0 <= qpos, key, INT_MIN)
        return carry

    lax.fori_loop(0, n_chunks, score_chunk, 0)

    def count(pred):
        def body(c, acc):
            w = jnp.where(pred(keys_ref[c], c), 1, 0)
            return acc + jnp.sum(w.reshape(kc // COUNT_ROWS, COUNT_ROWS, tq), axis=0)
        acc = lax.fori_loop(0, n_chunks, body, jnp.zeros((COUNT_ROWS, tq), I32))
        return jnp.sum(acc, axis=0, keepdims=True)

    k_eff = jnp.minimum(qpos + 1, topk)

    def class_max(c, gm):
        return jnp.maximum(gm, jnp.max(keys_ref[c].reshape(kc // topk, topk, tq), axis=0))

    gm = lax.fori_loop(0, n_chunks, class_max, jnp.full((topk, tq), INT_MIN, I32))
    hi = jnp.max(gm, axis=0, keepdims=True)
    lo = jnp.min(gm, axis=0, keepdims=True)
    span = hi - lo
    wide = jnp.max(jnp.where(jnp.logical_or(span < 0, span >= 2 ** 30), 1, 0)) > 0
    max_span = jnp.max(span)
    n_bits = jnp.where(wide, 32, sum((max_span >= 2 ** j).astype(I32) for j in range(30)))
    base = jnp.where(wide, INT_MIN, lo)

    def thr_bit(it, off):
        cand_off = off | lax.shift_left(np.int32(1), n_bits - 1 - it)
        cnt = count(lambda kb, c: kb >= base + cand_off)
        ok = jnp.logical_and(cnt >= k_eff, jnp.logical_or(wide, cand_off <= span))
        return jnp.where(ok, cand_off, off)

    thr = base + lax.fori_loop(0, n_bits, thr_bit, jnp.zeros((1, tq), I32))

    need = k_eff - count(lambda kb, c: kb > thr)
    n_ge = count(lambda kb, c: kb >= thr)
    has_tie = jnp.max(n_ge - k_eff) > 0

    def cut_bit(it, cut):
        cand = cut | lax.shift_left(np.int32(1), pos_bits - 1 - it)
        cnt = count(lambda kb, c: jnp.logical_and(kb == thr, c * kc + kpos0 < cand))
        return jnp.where(cnt < need, cand, cut)

    cut = lax.cond(has_tie,
                   lambda: lax.fori_loop(0, pos_bits, cut_bit, jnp.zeros((1, tq), I32)),
                   lambda: jnp.full((1, tq), 2 ** pos_bits - 1, I32))

    m_ref[...] = jnp.full(m_ref.shape, MASKED_LOGIT, F32)
    l_ref[...] = jnp.zeros(l_ref.shape, F32)
    acc_ref[...] = jnp.zeros(acc_ref.shape, F32)

    def attend_chunk(c, carry):
        kvc = kv_ref[pl.ds(pl.multiple_of(c * kc, kc), kc), :]
        kb = keys_ref[c]
        sel = jnp.logical_or(kb > thr, jnp.logical_and(kb == thr, c * kc + kpos0 <= cut))
        cap_ref[...] = jnp.where(sel, jnp.inf, MASKED_LOGIT)
        s_ref[...] = _dot(kvc[:, :A_HEAD_DIM], qt_ref[0])
        for h in range(nh):
            cols = slice(h * tq, (h + 1) * tq)
            s = jnp.minimum(s_ref[:, cols], cap_ref[...])
            m_old = m_ref[:, cols]
            m_new = jnp.maximum(m_old, jnp.max(s, axis=0, keepdims=True))
            alpha = jnp.exp2(m_old - m_new)
            p = jnp.exp2(s - m_new)
            l_ref[:, cols] = alpha * l_ref[:, cols] + jnp.sum(p, axis=0, keepdims=True)
            m_ref[:, cols] = m_new
            p_ref[:, cols] = p.astype(p_ref.dtype)
            acc_ref[:, cols] = acc_ref[:, cols] * alpha
        acc_ref[...] += _dot(vt_ref[c], p_ref[...])
        return carry

    lax.fori_loop(0, n_chunks, attend_chunk, 0)

    out_t = jnp.concatenate([acc_ref[:, h * tq:(h + 1) * tq] / l_ref[:, h * tq:(h + 1) * tq] for h in range(nh)], axis=0)
    o_ref[...] = out_t.T.astype(o_ref.dtype)


def _dsa(qt, qit, wit, kv, vt, kib, B, L, tq, kc):
    T = B * L
    nq = L // tq
    nkc = L // kc
    topk = min(TOPK_MAX, L // 4)
    pos_bits = max(1, int(np.ceil(np.log2(L))))
    assert kc % topk == 0 and topk % 8 == 0, (kc, topk)
    head_spec = pl.BlockSpec((1, 64, A_HEADS * tq), lambda b, i: (b * nq + i, 0, 0))
    seq_spec = pl.BlockSpec((L, LANES), lambda b, i: (b, 0))
    kern = functools.partial(_dsa_kernel, tq=tq, kc=kc, topk=topk, pos_bits=pos_bits)
    return pl.pallas_call(
        kern,
        grid=(B, nq),
        in_specs=[head_spec, head_spec, pl.BlockSpec((IDX_HEADS, tq), lambda b, i: (0, b * nq + i)),
                  seq_spec, pl.BlockSpec((nkc, 64, kc), lambda b, i: (b, 0, 0)), seq_spec],
        out_specs=pl.BlockSpec((tq, A_WIDTH), lambda b, i: (b * nq + i, 0)),
        out_shape=jax.ShapeDtypeStruct((T, A_WIDTH), MXU_DTYPE),
        scratch_shapes=[pltpu.VMEM((nkc, kc, tq), I32), pltpu.VMEM((kc, A_HEADS * tq), F32),
                        pltpu.VMEM((kc, A_HEADS * tq), MXU_DTYPE), pltpu.VMEM((kc, tq), F32),
                        pltpu.VMEM((1, A_HEADS * tq), F32), pltpu.VMEM((1, A_HEADS * tq), F32),
                        pltpu.VMEM((A_HEAD_DIM, A_HEADS * tq), F32)],
        compiler_params=_params(("parallel", "arbitrary"), 40),
        name="dsa",
    )(qt, qit, wit, kv, vt, kib)


POOL_HALO = 16


def _pool_kernel(ya_ref, u_ref, halo_ref, pw_ref, ps_ref, o_ref, *, n_pos):
    tm = u_ref.shape[0]
    i = pl.program_id(0)
    seq_start = (i % n_pos) == 0
    u = u_ref[...]
    halo = jnp.where(seq_start, 0.0, halo_ref[...])
    ue = jnp.concatenate([halo, u], axis=0)
    t = (i % n_pos) * tm + lax.broadcasted_iota(I32, (tm, 1), 0)
    o_ref[:, :A_WIDTH] = ya_ref[...]
    for gi, w in enumerate(POOL_WINDOWS):
        lanes = slice(gi * POOL_GROUP, (gi + 1) * POOL_GROUP)
        s = ue[:, lanes]
        span = 1
        while span < w:
            s = s + pltpu.roll(s, span, 0)
            span *= 2
        count = jnp.minimum(t + 1, w).astype(F32)
        pooled = s[POOL_HALO:, :] / count - u[:, lanes]
        y = _dot(pooled.astype(MXU_DTYPE), pw_ref[gi]) * ps_ref[:, lanes]
        o_ref[:, A_WIDTH + gi * POOL_GROUP: A_WIDTH + (gi + 1) * POOL_GROUP] = y.astype(o_ref.dtype)


def _pool(ya, u, pool_w, pool_scale, L, tm):
    T = u.shape[0]
    n_pos = L // tm
    row = lambda i: (i, 0)
    halo_map = lambda i: (jnp.maximum(i * (tm // POOL_HALO) - 1, 0), 0)
    return pl.pallas_call(
        functools.partial(_pool_kernel, n_pos=n_pos),
        grid=(T // tm,),
        in_specs=[pl.BlockSpec((tm, A_WIDTH), row), pl.BlockSpec((tm, POOL_WIDTH), row),
                  pl.BlockSpec((POOL_HALO, POOL_WIDTH), halo_map),
                  _resident(pool_w.shape), _resident((1, POOL_WIDTH))],
        out_specs=pl.BlockSpec((tm, A_WIDTH + POOL_WIDTH), row),
        out_shape=jax.ShapeDtypeStruct((T, A_WIDTH + POOL_WIDTH), MXU_DTYPE),
        compiler_params=_params(("parallel",), 32),
        name="pool",
    )(ya, u, u, pool_w, pool_scale)


def _odd_proj_kernel(h_ref, g_ref, w_ref, qkv_ref, z_ref, ba_ref):
    xn = _rms(h_ref[...], g_ref[...]).astype(MXU_DTYPE)
    n_qkv = qkv_ref.shape[1]
    n_z = z_ref.shape[1]
    step = 512
    for c in range(n_qkv // step):
        qkv_ref[:, c * step:(c + 1) * step] = _dot(xn, w_ref[:, c * step:(c + 1) * step])
    for c in range(n_z // step):
        z_ref[:, c * step:(c + 1) * step] = _dot(xn, w_ref[:, n_qkv + c * step: n_qkv + (c + 1) * step])
    ba_ref[...] = _dot(xn, w_ref[:, n_qkv + n_z:])


def _odd_proj(h2, g, w_all, tm):
    T, D = h2.shape
    N = w_all.shape[1]
    row = lambda i: (i, 0)
    return pl.pallas_call(
        _odd_proj_kernel,
        grid=(T // tm,),
        in_specs=[pl.BlockSpec((tm, D), row), _resident((1, D)), _resident((D, N))],
        out_specs=[pl.BlockSpec((tm, 3 * C_WIDTH), row), pl.BlockSpec((tm, C_WIDTH), row), pl.BlockSpec((tm, LANES), row)],
        out_shape=[jax.ShapeDtypeStruct((T, 3 * C_WIDTH), F32), jax.ShapeDtypeStruct((T, C_WIDTH), F32),
                   jax.ShapeDtypeStruct((T, LANES), F32)],
        compiler_params=_params(("parallel",), 48),
        name="odd_proj",
    )(h2, g, w_all)


def _gates_kernel(ba_ref, alog_ref, dtb_ref, gate_ref, cum_ref):
    tm = ba_ref.shape[0]
    x = ba_ref[...]
    lane = lax.broadcasted_iota(I32, x.shape, 1)
    beta = jax.nn.sigmoid(x)
    g = -jnp.exp(alog_ref[...]) * jnp.logaddexp(x + dtb_ref[...], 0.0)
    gate = jnp.where(lane < C_HEADS, beta, jnp.where(lane < 2 * C_HEADS, g, 0.0))
    gate_ref[...] = gate
    ii = lax.broadcasted_iota(I32, (CHUNK, CHUNK), 0)
    jj = lax.broadcasted_iota(I32, (CHUNK, CHUNK), 1)
    tri = (ii >= jj).astype(F32)
    for c in range(tm // CHUNK):
        rows = slice(c * CHUNK, (c + 1) * CHUNK)
        cum_ref[rows, :] = _dot3(tri, gate[rows, :])


def _gates(ba, alog_row, dtb_row, tm):
    T = ba.shape[0]
    row = lambda i: (i, 0)
    return pl.pallas_call(
        _gates_kernel,
        grid=(T // tm,),
        in_specs=[pl.BlockSpec((tm, LANES), row), _resident((1, LANES)), _resident((1, LANES))],
        out_specs=[pl.BlockSpec((tm, LANES), row), pl.BlockSpec((tm, LANES), row)],
        out_shape=[jax.ShapeDtypeStruct((T, LANES), F32), jax.ShapeDtypeStruct((T, LANES), F32)],
        compiler_params=_params(("parallel",), 16),
        name="gates",
    )(ba, alog_row, dtb_row)


CONV_HALO = 8


def _gdn_prep_kernel(xq_ref, xk_ref, xv_ref, hq_ref, hk_ref, hv_ref, wq_ref, wk_ref, wv_ref, gcol_ref, grow_ref,
                     kcd_ref, qd_ref, kdt_ref, attn_ref, val_ref):
    rows = xq_ref.shape[0]
    seq_start = pl.program_id(2) == 0

    def conv_silu(x_ref, halo_ref, w_ref):
        x = x_ref[...]
        halo = jnp.where(seq_start, 0.0, halo_ref[...])
        xe = jnp.concatenate([halo, x], axis=0)
        w = w_ref[...]
        y = w[CONV_WIDTH - 1:CONV_WIDTH, :] * x
        for j in range(CONV_WIDTH - 1):
            back = CONV_WIDTH - 1 - j
            y = y + w[j:j + 1, :] * pltpu.roll(xe, back, 0)[CONV_HALO:, :]
        return _silu(y)

    def l2n(x):
        return x * lax.rsqrt(jnp.sum(x * x, axis=-1, keepdims=True) + NORM_EPS)

    q = l2n(conv_silu(xq_ref, hq_ref, wq_ref)) * (C_HEAD_DIM ** -0.5)
    k = l2n(conv_silu(xk_ref, hk_ref, wk_ref))
    v = conv_silu(xv_ref, hv_ref, wv_ref)
    gcol = gcol_ref[0]
    grow = grow_ref[0, 0]

    ii = lax.broadcasted_iota(I32, (CHUNK, CHUNK), 0)
    jj = lax.broadcasted_iota(I32, (CHUNK, CHUNK), 1)
    tril = ii >= jj
    strict = ii > jj
    eye = (ii == jj).astype(F32)

    sls = [slice(c * CHUNK, (c + 1) * CHUNK) for c in range(rows // CHUNK)]
    beta = [gcol[s, 0:1] for s in sls]
    gc = [gcol[s, 1:2] for s in sls]
    decay = [jnp.exp(jnp.where(tril, g - grow[1:2, s], -jnp.inf)) for g, s in zip(gc, sls)]
    kb = [k[s].astype(MXU_DTYPE) for s in sls]
    kk = [_dot_nt(x, x) for x in kb]
    lmat = [jnp.where(strict, b * x * d, 0.0) for b, x, d in zip(beta, kk, decay)]
    tmat = [eye - jnp.where((ii >> 1) == (jj >> 1), x, 0.0) for x in lmat]
    shift = 1
    while (2 << shift) <= CHUNK:
        couple = jnp.logical_and((ii >> (shift + 1)) == (jj >> (shift + 1)), (ii >> shift) != (jj >> shift))
        cd = [_dot(jnp.where(couple, x, 0.0).astype(MXU_DTYPE), t.astype(MXU_DTYPE)) for x, t in zip(lmat, tmat)]
        tmat = [t - _dot(t.astype(MXU_DTYPE), y.astype(MXU_DTYPE)) for t, y in zip(tmat, cd)]
        shift += 1
    eg = [jnp.exp(g) for g in gc]
    for c, s in enumerate(sls):
        rhs = jnp.concatenate([v[s] * beta[c], k[s] * (beta[c] * eg[c])], axis=1).astype(MXU_DTYPE)
        solved = _dot(tmat[c].astype(MXU_DTYPE), rhs)
        val_ref[s, :] = solved[:, :C_HEAD_DIM]
        kcd_ref[s, :] = solved[:, C_HEAD_DIM:].astype(kcd_ref.dtype)
        attn_ref[s, :] = (_dot_nt(q[s].astype(MXU_DTYPE), kb[c]) * decay[c]).astype(attn_ref.dtype)
        qd_ref[s, :] = (q[s] * eg[c]).astype(qd_ref.dtype)
        g_last = gc[c][CHUNK - 1:CHUNK, :]
        kdt_ref[s, :] = (k[s] * jnp.exp(g_last - gc[c])).T.astype(kdt_ref.dtype)


def _gdn_prep(qkv, conv_w, gcol, grow, B, L, rows):
    T = B * L
    H = C_HEADS
    nr = L // rows
    hb = rows // CONV_HALO

    def x_spec(part):
        return pl.BlockSpec((rows, C_HEAD_DIM), lambda b, h, r: (b * nr + r, part * H + h))

    def halo_spec(part):
        return pl.BlockSpec((CONV_HALO, C_HEAD_DIM), lambda b, h, r: (jnp.maximum((b * nr + r) * hb - 1, 0), part * H + h))

    def w_spec(part):
        return pl.BlockSpec((CONV_WIDTH, C_HEAD_DIM), lambda b, h, r: (0, part * H + h))

    out_spec = pl.BlockSpec((rows, C_HEAD_DIM), lambda b, h, r: (b * nr + r, h))
    mm_shape = jax.ShapeDtypeStruct((T, C_WIDTH), MXU_DTYPE)
    return pl.pallas_call(
        _gdn_prep_kernel,
        grid=(B, H, nr),
        in_specs=[x_spec(0), x_spec(1), x_spec(2), halo_spec(0), halo_spec(1), halo_spec(2),
                  w_spec(0), w_spec(1), w_spec(2),
                  pl.BlockSpec((1, rows, 2), lambda b, h, r: (h, b * nr + r, 0)),
                  pl.BlockSpec((1, 1, 2, rows), lambda b, h, r: (b, h, 0, r))],
        out_specs=[out_spec] * 5,
        out_shape=[mm_shape, mm_shape, mm_shape, mm_shape, jax.ShapeDtypeStruct((T, C_WIDTH), F32)],
        compiler_params=_params(("parallel", "parallel", "parallel"), 32),
        name="gdn_prep",
    )(qkv, qkv, qkv, qkv, qkv, qkv, conv_w, conv_w, conv_w, gcol, grow)


def _gdn_scan_kernel(kcd_ref, qd_ref, kdt_ref, attn_ref, val_ref, grow_ref, z_ref, on_ref, y_ref, state_ref):
    rows = kcd_ref.shape[0]
    nh = C_HEADS

    @pl.when(pl.program_id(1) == 0)
    def _():
        state_ref[...] = jnp.zeros(state_ref.shape, F32)

    on = on_ref[...]
    for c in range(rows // CHUNK):
        s = slice(c * CHUNK, (c + 1) * CHUNK)
        cols = [slice(h * C_HEAD_DIM, (h + 1) * C_HEAD_DIM) for h in range(nh)]
        state = [state_ref[h] for h in range(nh)]
        both = [_dot(jnp.concatenate([kcd_ref[s, cl], qd_ref[s, cl]], axis=0), st.astype(MXU_DTYPE))
                for cl, st in zip(cols, state)]
        v_new = [(val_ref[s, cl] - b[:CHUNK]).astype(MXU_DTYPE) for cl, b in zip(cols, both)]
        res = [_dot(jnp.concatenate([attn_ref[s, cl], kdt_ref[s, cl]], axis=0), vn) for cl, vn in zip(cols, v_new)]
        for h in range(nh):
            g_last = grow_ref[0, h, 1:2, (c + 1) * CHUNK - 1:(c + 1) * CHUNK]
            state_ref[h] = state[h] * jnp.exp(g_last) + res[h][CHUNK:]
            o = both[h][CHUNK:] + res[h][:CHUNK]
            y = _rms(o, on) * _silu(z_ref[s, cols[h]])
            y_ref[s, cols[h]] = y.astype(y_ref.dtype)


def _gdn_scan(kcd, qd, kdt, attn, val, grow, z, o_norm, B, L, rows):
    T = B * L
    nr = L // rows
    row = pl.BlockSpec((rows, C_WIDTH), lambda b, r: (b * nr + r, 0))
    return pl.pallas_call(
        _gdn_scan_kernel,
        grid=(B, nr),
        in_specs=[row, row, row, row, row,
                  pl.BlockSpec((1, C_HEADS, 2, rows), lambda b, r: (b, 0, 0, r)),
                  row, pl.BlockSpec((1, C_HEAD_DIM), lambda b, r: (0, 0))],
        out_specs=row,
        out_shape=jax.ShapeDtypeStruct((T, C_WIDTH), MXU_DTYPE),
        scratch_shapes=[pltpu.VMEM((C_HEADS, C_HEAD_DIM, C_HEAD_DIM), F32)],
        compiler_params=_params(("parallel", "arbitrary"), 48),
        name="gdn_scan",
    )(kcd, qd, kdt, attn, val, grow, z, o_norm)


def _out_mlp_kernel(h_ref, y_ref, wo_ref, g_ref, w1_ref, w2_ref, fg_ref, o_ref, *, ff_chunk, final):
    h1 = h_ref[...] + _dot(y_ref[...], wo_ref[...])
    xn = _rms(h1, g_ref[...]).astype(MXU_DTYPE)
    acc = h1
    for c in range(w1_ref.shape[1] // ff_chunk):
        cols = slice(c * ff_chunk, (c + 1) * ff_chunk)
        a = jnp.square(jnp.maximum(_dot(xn, w1_ref[:, cols]), 0.0)).astype(MXU_DTYPE)
        acc = acc + _dot(a, w2_ref[cols, :])
    if final:
        acc = _rms(acc, fg_ref[...])
    o_ref[...] = acc


def _out_mlp(h2, y, w_out, g, w1, w2, final_g, final, tm):
    T, D = h2.shape
    dff = w1.shape[1]
    row = lambda i: (i, 0)
    return pl.pallas_call(
        functools.partial(_out_mlp_kernel, ff_chunk=512, final=final),
        grid=(T // tm,),
        in_specs=[pl.BlockSpec((tm, D), row), pl.BlockSpec((tm, y.shape[1]), row), _resident(w_out.shape),
                  _resident((1, D)), _resident((D, dff)), _resident((dff, D)), _resident((1, D))],
        out_specs=pl.BlockSpec((tm, D), row),
        out_shape=jax.ShapeDtypeStruct((T, D), F32),
        compiler_params=_params(("parallel",), 48),
        name="out_mlp",
    )(h2, y, w_out, g, w1, w2, final_g)


def _rope_tables(L):
    inv = ROPE_THETA ** (-jnp.arange(0, A_HEAD_DIM, 2, dtype=F32) / A_HEAD_DIM)
    ang = jnp.arange(L, dtype=F32)[:, None] * inv[None, :]
    cos, sin = jnp.cos(ang), jnp.sin(ang)
    one, zero = jnp.ones((L, 64), F32), jnp.zeros((L, 64), F32)
    cos2 = jnp.concatenate([cos, cos, cos, cos], axis=1)
    sin2 = jnp.concatenate([-sin, sin, -sin, sin], axis=1)
    cosk = jnp.concatenate([cos, cos, one], axis=1)
    sink = jnp.concatenate([-sin, sin, zero], axis=1)
    return cos2, sin2, cosk, sink


def _pick_tile(L, want):
    t = min(want, L)
    while L % t:
        t //= 2
    return t


def kernel(x, mix_norm, mlp_norm, w_ff1, w_ff2, ev_w_in, ev_kv_norm, ev_w_uk, ev_w_uv, ev_pool_w, ev_pool_scale, ev_w_out, od_w_in, od_conv_w, od_a_log, od_dt_bias, od_o_norm, od_w_out, final_norm):
    B, L, D = x.shape
    T = B * L
    depth = mix_norm.shape[0]
    mm = MXU_DTYPE
    h = x.reshape(T, D)
    tabs = _rope_tables(L)
    tm = _pick_tile(L, 512)
    tm_mlp = _pick_tile(L, 512)
    tq = _pick_tile(L, 128)
    kc = _pick_tile(L, 512)
    gdn_rows = _pick_tile(L, 512)
    fg = final_norm.reshape(1, D)

    for layer in range(depth):
        j = layer // 2
        g_mix = mix_norm[layer].reshape(1, D)
        if layer % 2 == 0:
            w = ev_w_in[j]
            c0 = A_WIDTH
            c1 = c0 + A_KV_RANK
            c2 = c1 + IDX_HEADS * IDX_DIM
            c3 = c2 + IDX_DIM
            c4 = c3 + IDX_HEADS
            pad = jnp.zeros((D, LANES - IDX_DIM - IDX_HEADS), w.dtype)
            w_all = jnp.concatenate([w[:, :c0], w[:, c1:c2], w[:, c4:], w[:, c0:c1], w[:, c2:c3], w[:, c3:c4], pad], axis=1).astype(mm)
            wkv = jnp.concatenate([ev_w_uk[j], ev_w_uv[j]], axis=1).astype(mm)
            qt, qit, u, kv, vt, kib, wit = _even_proj(h, g_mix, w_all, ev_kv_norm[j].reshape(1, -1), wkv, tabs, L, tm, tq, kc)
            ya = _dsa(qt, qit, wit, kv, vt, kib, B, L, tq, kc)
            y = _pool(ya, u, ev_pool_w[j].astype(mm), ev_pool_scale[j].reshape(1, -1), L, tm)
            w_out = ev_w_out[j].astype(mm)
        else:
            w = od_w_in[j]
            pad = jnp.zeros((D, LANES - 2 * C_HEADS), w.dtype)
            w_all = jnp.concatenate([w, pad], axis=1).astype(mm)
            qkv, z, ba = _odd_proj(h, g_mix, w_all, tm)
            zero8 = jnp.zeros((C_HEADS,), F32)
            lane_pad = jnp.zeros((LANES - 2 * C_HEADS,), F32)
            alog_row = jnp.concatenate([zero8, od_a_log[j], lane_pad]).reshape(1, LANES)
            dtb_row = jnp.concatenate([zero8, od_dt_bias[j], lane_pad]).reshape(1, LANES)
            gate, cum = _gates(ba, alog_row, dtb_row, tm)
            beta = gate[:, :C_HEADS]
            gcum = cum[:, C_HEADS:2 * C_HEADS]
            gcol = jnp.stack([beta.T, gcum.T], axis=-1)
            grow = jnp.stack([beta.reshape(B, L, C_HEADS), gcum.reshape(B, L, C_HEADS)], axis=1)
            grow = jnp.transpose(grow, (0, 3, 1, 2))
            kcd, qd, kdt, attn, val = _gdn_prep(qkv, od_conv_w[j], gcol, grow, B, L, gdn_rows)
            y = _gdn_scan(kcd, qd, kdt, attn, val, grow, z, od_o_norm[j].reshape(1, -1), B, L, gdn_rows)
            w_out = od_w_out[j].astype(mm)
        h = _out_mlp(h, y, w_out, mlp_norm[layer].reshape(1, D), w_ff1[layer].astype(mm), w_ff2[layer].astype(mm),
                     fg, layer == depth - 1, tm_mlp)
    return h.reshape(B, L, D)
```

```python
import functools

import numpy as np
import jax
import jax.numpy as jnp
from jax import lax
from jax.experimental import pallas as pl
from jax.experimental.pallas import tpu as pltpu

F32 = jnp.float32
I32 = jnp.int32
MXU_DTYPE = jnp.bfloat16

A_HEADS = 8
A_HEAD_DIM = 64
A_WIDTH = A_HEADS * A_HEAD_DIM
A_KV_RANK = 128
IDX_HEADS = 8
IDX_DIM = 64
TOPK_MAX = 256
POOL_WINDOWS = (2, 4, 8, 16)
POOL_GROUP = 128
POOL_WIDTH = POOL_GROUP * len(POOL_WINDOWS)
C_HEADS = 8
C_HEAD_DIM = 128
C_WIDTH = C_HEADS * C_HEAD_DIM
CONV_WIDTH = 4
CHUNK = 128
ROPE_THETA = 10000.0
NORM_EPS = 1e-6

LANES = 128
INT_MIN = np.int32(-2 ** 31)
MASKED_LOGIT = -1e30
COUNT_ROWS = 32
LOG2_E = float(np.log2(np.e))

NT_DIMS = (((1,), (1,)), ((), ()))


def _params(semantics, vmem_mb):
    return pltpu.CompilerParams(dimension_semantics=semantics, vmem_limit_bytes=vmem_mb * 1024 * 1024)


def _dot(a, b):
    return jnp.dot(a, b, preferred_element_type=F32)


def _dot_nt(a, b):
    return lax.dot_general(a, b, NT_DIMS, preferred_element_type=F32)


def _split(a):
    hi = a.astype(MXU_DTYPE)
    lo = (a - hi.astype(F32)).astype(MXU_DTYPE)
    return hi, lo


def _dot3(a, b):
    ah, al = _split(a)
    bh, bl = _split(b)
    return _dot(ah, bh) + (_dot(ah, bl) + _dot(al, bh))


def _rms(x, g):
    return x * lax.rsqrt(jnp.mean(x * x, axis=-1, keepdims=True) + NORM_EPS) * g


def _silu(x):
    return x * jax.nn.sigmoid(x)


def _resident(shape):
    nd = len(shape)
    return pl.BlockSpec(shape, lambda *_: (0,) * nd, pipeline_mode=pl.Buffered(1))


def _rope(x, cos, sin_signed, first_half):
    partner = jnp.where(first_half, pltpu.roll(x, LANES - 32, 1), pltpu.roll(x, 32, 1))
    return x * cos + partner * sin_signed


def _even_proj_kernel(h_ref, g_ref, w_ref, kvn_ref, wkv_ref, cos_ref, sin_ref, cosk_ref, sink_ref,
                      qt_ref, qit_ref, u_ref, kv_ref, vt_ref, kib_ref, wit_ref, *, tq, kc):
    tm = h_ref.shape[0]
    xn = _rms(h_ref[...], g_ref[...]).astype(MXU_DTYPE)
    p = _dot(xn, w_ref[...])
    lane = lax.broadcasted_iota(I32, (tm, LANES), 1)
    first_half = (lane % 64) < 32
    cos, sin = cos_ref[...], sin_ref[...]
    cosk, sink = cosk_ref[...], sink_ref[...]

    def heads_out(o_ref, base, scale):
        for s in range(4):
            r = _rope(p[:, base + s * LANES: base + (s + 1) * LANES], cos, sin, first_half)
            if scale != 1.0:
                r = r * scale
            for j in range(tm // tq):
                rt = r[j * tq:(j + 1) * tq, :].T.astype(o_ref.dtype)
                o_ref[j, :, 2 * s * tq:(2 * s + 1) * tq] = rt[:64, :]
                o_ref[j, :, (2 * s + 1) * tq:(2 * s + 2) * tq] = rt[64:, :]

    heads_out(qt_ref, 0, (A_HEAD_DIM ** -0.5) * LOG2_E)
    heads_out(qit_ref, A_WIDTH, 1.0)
    u_ref[...] = p[:, 1024:1536]
    ckv = _rms(p[:, 1536:1664], kvn_ref[...]).astype(MXU_DTYPE)
    kv = _rope(_dot(ckv, wkv_ref[...]), cosk, sink, first_half)
    kv_ref[...] = kv.astype(kv_ref.dtype)
    for j in range(tm // kc):
        vt_ref[j] = kv[j * kc:(j + 1) * kc, :].T[64:, :].astype(vt_ref.dtype)
    kiw = _rope(p[:, 1664:1792], cosk, sink, first_half)
    kib_ref[...] = kiw.astype(kib_ref.dtype)
    wit = kiw.T[64:64 + IDX_HEADS, :]
    wit_ref[...] = (wit * (IDX_HEADS ** -0.5)) * (IDX_DIM ** -0.5)


def _even_proj(h2, g, w_all, kvn, wkv, tabs, L, tm, tq, kc):
    T, D = h2.shape
    n_pos = L // tm
    N = w_all.shape[1]
    row = lambda i: (i, 0)
    pos = lambda i: (i % n_pos, 0)
    tab_spec = pl.BlockSpec((tm, LANES), pos)
    head_spec = pl.BlockSpec((tm // tq, 64, A_HEADS * tq), lambda i: (i, 0, 0))
    head_shape = jax.ShapeDtypeStruct((T // tq, 64, A_HEADS * tq), MXU_DTYPE)
    return pl.pallas_call(
        functools.partial(_even_proj_kernel, tq=tq, kc=kc),
        grid=(T // tm,),
        in_specs=[pl.BlockSpec((tm, D), row), _resident((1, D)), _resident((D, N)), _resident((1, A_KV_RANK)),
                  _resident((A_KV_RANK, LANES)), tab_spec, tab_spec, tab_spec, tab_spec],
        out_specs=[head_spec, head_spec, pl.BlockSpec((tm, POOL_WIDTH), row), pl.BlockSpec((tm, LANES), row),
                   pl.BlockSpec((tm // kc, 64, kc), lambda i: (i, 0, 0)), pl.BlockSpec((tm, LANES), row),
                   pl.BlockSpec((IDX_HEADS, tm), lambda i: (0, i))],
        out_shape=[head_shape, head_shape, jax.ShapeDtypeStruct((T, POOL_WIDTH), F32),
                   jax.ShapeDtypeStruct((T, LANES), MXU_DTYPE), jax.ShapeDtypeStruct((T // kc, 64, kc), MXU_DTYPE),
                   jax.ShapeDtypeStruct((T, LANES), MXU_DTYPE), jax.ShapeDtypeStruct((IDX_HEADS, T), F32)],
        compiler_params=_params(("parallel",), 40),
        name="even_proj",
    )(h2, g, w_all, kvn, wkv, *tabs)


def _dsa_kernel(qt_ref, qit_ref, wit_ref, kv_ref, vt_ref, kib_ref, o_ref,
                keys_ref, s_ref, p_ref, m_ref, l_ref, a_ref, acc_ref, *, tq, kc, topk, pos_bits):
    i = pl.program_id(1)
    n_chunks = ((i + 1) * tq + kc - 1) // kc
    nh = A_HEADS
    qpos = i * tq + lax.broadcasted_iota(I32, (1, tq), 1)
    kpos0 = lax.broadcasted_iota(I32, (kc, tq), 0)
    wit = wit_ref[...]

    def score_chunk(c, carry):
        kic = kib_ref[pl.ds(pl.multiple_of(c * kc, kc), kc), :]
        s_ref[...] = _dot(kic[:, :IDX_DIM], qit_ref[0])
        score = jnp.zeros((kc, tq), F32)
        for h in range(nh):
            score = score + jnp.maximum(s_ref[:, h * tq:(h + 1) * tq], 0.0) * wit[h:h + 1, :]
        bits = lax.bitcast_convert_type(score, I32)
        mag = bits & np.int32(0x7FFFFFFF)
        key = jnp.where(bits < 0, -mag, mag)
        keys_ref[c] = jnp.where(c * kc + kpos0 <= qpos, key, INT_MIN)
        return carry

    lax.fori_loop(0, n_chunks, score_chunk, 0)

    def count(pred):
        def body(c, acc):
            w = jnp.where(pred(keys_ref[c], c), 1, 0)
            return acc + jnp.sum(w.reshape(kc // COUNT_ROWS, COUNT_ROWS, tq), axis=0)
        acc = lax.fori_loop(0, n_chunks, body, jnp.zeros((COUNT_ROWS, tq), I32))
        return jnp.sum(acc, axis=0, keepdims=True)

    k_eff = jnp.minimum(qpos + 1, topk)

    def thr_bit(it, thr):
        cand = thr ^ lax.shift_left(np.int32(1), 31 - it)
        cnt = count(lambda kb, c: kb >= cand)
        return jnp.where(cnt >= k_eff, cand, thr)

    thr = lax.fori_loop(0, 32, thr_bit, jnp.full((1, tq), INT_MIN, I32))

    need = k_eff - count(lambda kb, c: kb > thr)
    n_ge = count(lambda kb, c: kb >= thr)
    has_tie = jnp.max(n_ge - k_eff) > 0

    def cut_bit(it, cut):
        cand = cut | lax.shift_left(np.int32(1), pos_bits - 1 - it)
        cnt = count(lambda kb, c: jnp.logical_and(kb == thr, c * kc + kpos0 < cand))
        return jnp.where(cnt < need, cand, cut)

    cut = lax.cond(has_tie,
                   lambda: lax.fori_loop(0, pos_bits, cut_bit, jnp.zeros((1, tq), I32)),
                   lambda: jnp.full((1, tq), 2 ** pos_bits - 1, I32))

    m_ref[...] = jnp.full(m_ref.shape, MASKED_LOGIT, F32)
    l_ref[...] = jnp.zeros(l_ref.shape, F32)
    acc_ref[...] = jnp.zeros(acc_ref.shape, F32)

    def attend_chunk(c, carry):
        kvc = kv_ref[pl.ds(pl.multiple_of(c * kc, kc), kc), :][:, :A_HEAD_DIM]
        kb = keys_ref[c]
        sel = jnp.logical_or(kb > thr, jnp.logical_and(kb == thr, c * kc + kpos0 <= cut))
        cap = jnp.where(sel, jnp.inf, MASKED_LOGIT)
        for h in range(nh):
            cols = slice(h * tq, (h + 1) * tq)
            s = jnp.minimum(_dot(kvc, qt_ref[0, :, cols]), cap)
            s_ref[:, cols] = s
            m_old = m_ref[:, cols]
            m_new = jnp.maximum(m_old, jnp.max(s, axis=0, keepdims=True))
            m_ref[:, cols] = m_new
            a_ref[:, cols] = jnp.exp2(m_old - m_new)
        for h in range(nh):
            cols = slice(h * tq, (h + 1) * tq)
            p = jnp.exp2(s_ref[:, cols] - m_ref[:, cols])
            l_ref[:, cols] = a_ref[:, cols] * l_ref[:, cols] + jnp.sum(p, axis=0, keepdims=True)
            p_ref[:, cols] = p.astype(p_ref.dtype)
        acc_ref[...] = acc_ref[...] * a_ref[...] + _dot(vt_ref[c], p_ref[...])
        return carry

    lax.fori_loop(0, n_chunks, attend_chunk, 0)

    out_t = jnp.concatenate([acc_ref[:, h * tq:(h + 1) * tq] / l_ref[:, h * tq:(h + 1) * tq] for h in range(nh)], axis=0)
    o_ref[...] = out_t.T.astype(o_ref.dtype)


def _dsa(qt, qit, wit, kv, vt, kib, B, L, tq, kc):
    T = B * L
    nq = L // tq
    nkc = L // kc
    topk = min(TOPK_MAX, L // 4)
    pos_bits = max(1, int(np.ceil(np.log2(L))))
    head_spec = pl.BlockSpec((1, 64, A_HEADS * tq), lambda b, i: (b * nq + i, 0, 0))
    seq_spec = pl.BlockSpec((L, LANES), lambda b, i: (b, 0))
    kern = functools.partial(_dsa_kernel, tq=tq, kc=kc, topk=topk, pos_bits=pos_bits)
    return pl.pallas_call(
        kern,
        grid=(B, nq),
        in_specs=[head_spec, head_spec, pl.BlockSpec((IDX_HEADS, tq), lambda b, i: (0, b * nq + i)),
                  seq_spec, pl.BlockSpec((nkc, 64, kc), lambda b, i: (b, 0, 0)), seq_spec],
        out_specs=pl.BlockSpec((tq, A_WIDTH), lambda b, i: (b * nq + i, 0)),
        out_shape=jax.ShapeDtypeStruct((T, A_WIDTH), MXU_DTYPE),
        scratch_shapes=[pltpu.VMEM((nkc, kc, tq), I32), pltpu.VMEM((kc, A_HEADS * tq), F32),
                        pltpu.VMEM((kc, A_HEADS * tq), MXU_DTYPE),
                        pltpu.VMEM((1, A_HEADS * tq), F32), pltpu.VMEM((1, A_HEADS * tq), F32),
                        pltpu.VMEM((1, A_HEADS * tq), F32), pltpu.VMEM((A_HEAD_DIM, A_HEADS * tq), F32)],
        compiler_params=_params(("parallel", "arbitrary"), 40),
        name="dsa",
    )(qt, qit, wit, kv, vt, kib)


POOL_HALO = 16


def _pool_kernel(ya_ref, u_ref, halo_ref, pw_ref, ps_ref, o_ref, *, n_pos):
    tm = u_ref.shape[0]
    i = pl.program_id(0)
    seq_start = (i % n_pos) == 0
    u = u_ref[...]
    halo = jnp.where(seq_start, 0.0, halo_ref[...])
    ue = jnp.concatenate([halo, u], axis=0)
    t = (i % n_pos) * tm + lax.broadcasted_iota(I32, (tm, 1), 0)
    o_ref[:, :A_WIDTH] = ya_ref[...]
    for gi, w in enumerate(POOL_WINDOWS):
        lanes = slice(gi * POOL_GROUP, (gi + 1) * POOL_GROUP)
        s = ue[:, lanes]
        span = 1
        while span < w:
            s = s + pltpu.roll(s, span, 0)
            span *= 2
        count = jnp.minimum(t + 1, w).astype(F32)
        pooled = s[POOL_HALO:, :] / count - u[:, lanes]
        y = _dot(pooled.astype(MXU_DTYPE), pw_ref[gi]) * ps_ref[:, lanes]
        o_ref[:, A_WIDTH + gi * POOL_GROUP: A_WIDTH + (gi + 1) * POOL_GROUP] = y.astype(o_ref.dtype)


def _pool(ya, u, pool_w, pool_scale, L, tm):
    T = u.shape[0]
    n_pos = L // tm
    row = lambda i: (i, 0)
    halo_map = lambda i: (jnp.maximum(i * (tm // POOL_HALO) - 1, 0), 0)
    return pl.pallas_call(
        functools.partial(_pool_kernel, n_pos=n_pos),
        grid=(T // tm,),
        in_specs=[pl.BlockSpec((tm, A_WIDTH), row), pl.BlockSpec((tm, POOL_WIDTH), row),
                  pl.BlockSpec((POOL_HALO, POOL_WIDTH), halo_map),
                  _resident(pool_w.shape), _resident((1, POOL_WIDTH))],
        out_specs=pl.BlockSpec((tm, A_WIDTH + POOL_WIDTH), row),
        out_shape=jax.ShapeDtypeStruct((T, A_WIDTH + POOL_WIDTH), MXU_DTYPE),
        compiler_params=_params(("parallel",), 32),
        name="pool",
    )(ya, u, u, pool_w, pool_scale)


def _odd_proj_kernel(h_ref, g_ref, w_ref, qkv_ref, z_ref, ba_ref):
    xn = _rms(h_ref[...], g_ref[...]).astype(MXU_DTYPE)
    n_qkv = qkv_ref.shape[1]
    n_z = z_ref.shape[1]
    step = 512
    for c in range(n_qkv // step):
        qkv_ref[:, c * step:(c + 1) * step] = _dot(xn, w_ref[:, c * step:(c + 1) * step])
    for c in range(n_z // step):
        z_ref[:, c * step:(c + 1) * step] = _dot(xn, w_ref[:, n_qkv + c * step: n_qkv + (c + 1) * step])
    ba_ref[...] = _dot(xn, w_ref[:, n_qkv + n_z:])


def _odd_proj(h2, g, w_all, tm):
    T, D = h2.shape
    N = w_all.shape[1]
    row = lambda i: (i, 0)
    return pl.pallas_call(
        _odd_proj_kernel,
        grid=(T // tm,),
        in_specs=[pl.BlockSpec((tm, D), row), _resident((1, D)), _resident((D, N))],
        out_specs=[pl.BlockSpec((tm, 3 * C_WIDTH), row), pl.BlockSpec((tm, C_WIDTH), row), pl.BlockSpec((tm, LANES), row)],
        out_shape=[jax.ShapeDtypeStruct((T, 3 * C_WIDTH), F32), jax.ShapeDtypeStruct((T, C_WIDTH), F32),
                   jax.ShapeDtypeStruct((T, LANES), F32)],
        compiler_params=_params(("parallel",), 48),
        name="odd_proj",
    )(h2, g, w_all)


def _gates_kernel(ba_ref, alog_ref, dtb_ref, gate_ref, cum_ref):
    tm = ba_ref.shape[0]
    x = ba_ref[...]
    lane = lax.broadcasted_iota(I32, x.shape, 1)
    beta = jax.nn.sigmoid(x)
    g = -jnp.exp(alog_ref[...]) * jnp.logaddexp(x + dtb_ref[...], 0.0)
    gate = jnp.where(lane < C_HEADS, beta, jnp.where(lane < 2 * C_HEADS, g, 0.0))
    gate_ref[...] = gate
    ii = lax.broadcasted_iota(I32, (CHUNK, CHUNK), 0)
    jj = lax.broadcasted_iota(I32, (CHUNK, CHUNK), 1)
    tri = (ii >= jj).astype(F32)
    for c in range(tm // CHUNK):
        rows = slice(c * CHUNK, (c + 1) * CHUNK)
        cum_ref[rows, :] = _dot3(tri, gate[rows, :])


def _gates(ba, alog_row, dtb_row, tm):
    T = ba.shape[0]
    row = lambda i: (i, 0)
    return pl.pallas_call(
        _gates_kernel,
        grid=(T // tm,),
        in_specs=[pl.BlockSpec((tm, LANES), row), _resident((1, LANES)), _resident((1, LANES))],
        out_specs=[pl.BlockSpec((tm, LANES), row), pl.BlockSpec((tm, LANES), row)],
        out_shape=[jax.ShapeDtypeStruct((T, LANES), F32), jax.ShapeDtypeStruct((T, LANES), F32)],
        compiler_params=_params(("parallel",), 16),
        name="gates",
    )(ba, alog_row, dtb_row)


CONV_HALO = 8


def _gdn_prep_kernel(xq_ref, xk_ref, xv_ref, hq_ref, hk_ref, hv_ref, wq_ref, wk_ref, wv_ref, gcol_ref, grow_ref,
                     kcd_ref, qd_ref, kdt_ref, attn_ref, val_ref):
    rows = xq_ref.shape[0]
    seq_start = pl.program_id(2) == 0

    def conv_silu(x_ref, halo_ref, w_ref):
        x = x_ref[...]
        halo = jnp.where(seq_start, 0.0, halo_ref[...])
        xe = jnp.concatenate([halo, x], axis=0)
        w = w_ref[...]
        y = w[CONV_WIDTH - 1:CONV_WIDTH, :] * x
        for j in range(CONV_WIDTH - 1):
            back = CONV_WIDTH - 1 - j
            y = y + w[j:j + 1, :] * pltpu.roll(xe, back, 0)[CONV_HALO:, :]
        return _silu(y)

    def l2n(x):
        return x * lax.rsqrt(jnp.sum(x * x, axis=-1, keepdims=True) + NORM_EPS)

    q = l2n(conv_silu(xq_ref, hq_ref, wq_ref)) * (C_HEAD_DIM ** -0.5)
    k = l2n(conv_silu(xk_ref, hk_ref, wk_ref))
    v = conv_silu(xv_ref, hv_ref, wv_ref)
    gcol = gcol_ref[0]
    grow = grow_ref[0, 0]

    ii = lax.broadcasted_iota(I32, (CHUNK, CHUNK), 0)
    jj = lax.broadcasted_iota(I32, (CHUNK, CHUNK), 1)
    tril = ii >= jj
    strict = ii > jj
    eye = (ii == jj).astype(F32)

    sls = [slice(c * CHUNK, (c + 1) * CHUNK) for c in range(rows // CHUNK)]
    beta = [gcol[s, 0:1] for s in sls]
    gc = [gcol[s, 1:2] for s in sls]
    decay = [jnp.exp(jnp.where(tril, g - grow[1:2, s], -jnp.inf)) for g, s in zip(gc, sls)]
    kb = [k[s].astype(MXU_DTYPE) for s in sls]
    kk = [_dot_nt(x, x) for x in kb]
    lmat = [jnp.where(strict, b * x * d, 0.0) for b, x, d in zip(beta, kk, decay)]
    tmat = [eye - jnp.where((ii >> 1) == (jj >> 1), x, 0.0) for x in lmat]
    shift = 1
    while (2 << shift) <= CHUNK:
        couple = jnp.logical_and((ii >> (shift + 1)) == (jj >> (shift + 1)), (ii >> shift) != (jj >> shift))
        cd = [_dot(jnp.where(couple, x, 0.0).astype(MXU_DTYPE), t.astype(MXU_DTYPE)) for x, t in zip(lmat, tmat)]
        tmat = [t - _dot(t.astype(MXU_DTYPE), y.astype(MXU_DTYPE)) for t, y in zip(tmat, cd)]
        shift += 1
    eg = [jnp.exp(g) for g in gc]
    for c, s in enumerate(sls):
        rhs = jnp.concatenate([v[s] * beta[c], k[s] * (beta[c] * eg[c])], axis=1).astype(MXU_DTYPE)
        solved = _dot(tmat[c].astype(MXU_DTYPE), rhs)
        val_ref[s, :] = solved[:, :C_HEAD_DIM]
        kcd_ref[s, :] = solved[:, C_HEAD_DIM:].astype(kcd_ref.dtype)
        attn_ref[s, :] = (_dot_nt(q[s].astype(MXU_DTYPE), kb[c]) * decay[c]).astype(attn_ref.dtype)
        qd_ref[s, :] = (q[s] * eg[c]).astype(qd_ref.dtype)
        g_last = gc[c][CHUNK - 1:CHUNK, :]
        kdt_ref[s, :] = (k[s] * jnp.exp(g_last - gc[c])).T.astype(kdt_ref.dtype)


def _gdn_prep(qkv, conv_w, gcol, grow, B, L, rows):
    T = B * L
    H = C_HEADS
    nr = L // rows
    hb = rows // CONV_HALO

    def x_spec(part):
        return pl.BlockSpec((rows, C_HEAD_DIM), lambda b, h, r: (b * nr + r, part * H + h))

    def halo_spec(part):
        return pl.BlockSpec((CONV_HALO, C_HEAD_DIM), lambda b, h, r: (jnp.maximum((b * nr + r) * hb - 1, 0), part * H + h))

    def w_spec(part):
        return pl.BlockSpec((CONV_WIDTH, C_HEAD_DIM), lambda b, h, r: (0, part * H + h))

    out_spec = pl.BlockSpec((rows, C_HEAD_DIM), lambda b, h, r: (b * nr + r, h))
    mm_shape = jax.ShapeDtypeStruct((T, C_WIDTH), MXU_DTYPE)
    return pl.pallas_call(
        _gdn_prep_kernel,
        grid=(B, H, nr),
        in_specs=[x_spec(0), x_spec(1), x_spec(2), halo_spec(0), halo_spec(1), halo_spec(2),
                  w_spec(0), w_spec(1), w_spec(2),
                  pl.BlockSpec((1, rows, 2), lambda b, h, r: (h, b * nr + r, 0)),
                  pl.BlockSpec((1, 1, 2, rows), lambda b, h, r: (b, h, 0, r))],
        out_specs=[out_spec] * 5,
        out_shape=[mm_shape, mm_shape, mm_shape, mm_shape, jax.ShapeDtypeStruct((T, C_WIDTH), F32)],
        compiler_params=_params(("parallel", "parallel", "parallel"), 32),
        name="gdn_prep",
    )(qkv, qkv, qkv, qkv, qkv, qkv, conv_w, conv_w, conv_w, gcol, grow)


def _gdn_scan_kernel(kcd_ref, qd_ref, kdt_ref, attn_ref, val_ref, grow_ref, z_ref, on_ref, y_ref, state_ref):
    rows = kcd_ref.shape[0]
    nh = C_HEADS

    @pl.when(pl.program_id(1) == 0)
    def _():
        state_ref[...] = jnp.zeros(state_ref.shape, F32)

    on = on_ref[...]
    for c in range(rows // CHUNK):
        s = slice(c * CHUNK, (c + 1) * CHUNK)
        cols = [slice(h * C_HEAD_DIM, (h + 1) * C_HEAD_DIM) for h in range(nh)]
        state = [state_ref[h] for h in range(nh)]
        both = [_dot(jnp.concatenate([kcd_ref[s, cl], qd_ref[s, cl]], axis=0), st.astype(MXU_DTYPE))
                for cl, st in zip(cols, state)]
        v_new = [(val_ref[s, cl] - b[:CHUNK]).astype(MXU_DTYPE) for cl, b in zip(cols, both)]
        res = [_dot(jnp.concatenate([attn_ref[s, cl], kdt_ref[s, cl]], axis=0), vn) for cl, vn in zip(cols, v_new)]
        for h in range(nh):
            g_last = grow_ref[0, h, 1:2, (c + 1) * CHUNK - 1:(c + 1) * CHUNK]
            state_ref[h] = state[h] * jnp.exp(g_last) + res[h][CHUNK:]
            o = both[h][CHUNK:] + res[h][:CHUNK]
            y = _rms(o, on) * _silu(z_ref[s, cols[h]])
            y_ref[s, cols[h]] = y.astype(y_ref.dtype)


def _gdn_scan(kcd, qd, kdt, attn, val, grow, z, o_norm, B, L, rows):
    T = B * L
    nr = L // rows
    row = pl.BlockSpec((rows, C_WIDTH), lambda b, r: (b * nr + r, 0))
    return pl.pallas_call(
        _gdn_scan_kernel,
        grid=(B, nr),
        in_specs=[row, row, row, row, row,
                  pl.BlockSpec((1, C_HEADS, 2, rows), lambda b, r: (b, 0, 0, r)),
                  row, pl.BlockSpec((1, C_HEAD_DIM), lambda b, r: (0, 0))],
        out_specs=row,
        out_shape=jax.ShapeDtypeStruct((T, C_WIDTH), MXU_DTYPE),
        scratch_shapes=[pltpu.VMEM((C_HEADS, C_HEAD_DIM, C_HEAD_DIM), F32)],
        compiler_params=_params(("parallel", "arbitrary"), 48),
        name="gdn_scan",
    )(kcd, qd, kdt, attn, val, grow, z, o_norm)


def _out_mlp_kernel(h_ref, y_ref, wo_ref, g_ref, w1_ref, w2_ref, fg_ref, o_ref, *, ff_chunk, final):
    h1 = h_ref[...] + _dot(y_ref[...], wo_ref[...])
    xn = _rms(h1, g_ref[...]).astype(MXU_DTYPE)
    acc = h1
    for c in range(w1_ref.shape[1] // ff_chunk):
        cols = slice(c * ff_chunk, (c + 1) * ff_chunk)
        a = jnp.square(jnp.maximum(_dot(xn, w1_ref[:, cols]), 0.0)).astype(MXU_DTYPE)
        acc = acc + _dot(a, w2_ref[cols, :])
    if final:
        acc = _rms(acc, fg_ref[...])
    o_ref[...] = acc


def _out_mlp(h2, y, w_out, g, w1, w2, final_g, final, tm):
    T, D = h2.shape
    dff = w1.shape[1]
    row = lambda i: (i, 0)
    return pl.pallas_call(
        functools.partial(_out_mlp_kernel, ff_chunk=512, final=final),
        grid=(T // tm,),
        in_specs=[pl.BlockSpec((tm, D), row), pl.BlockSpec((tm, y.shape[1]), row), _resident(w_out.shape),
                  _resident((1, D)), _resident((D, dff)), _resident((dff, D)), _resident((1, D))],
        out_specs=pl.BlockSpec((tm, D), row),
        out_shape=jax.ShapeDtypeStruct((T, D), F32),
        compiler_params=_params(("parallel",), 48),
        name="out_mlp",
    )(h2, y, w_out, g, w1, w2, final_g)


def _rope_tables(L):
    inv = ROPE_THETA ** (-jnp.arange(0, A_HEAD_DIM, 2, dtype=F32) / A_HEAD_DIM)
    ang = jnp.arange(L, dtype=F32)[:, None] * inv[None, :]
    cos, sin = jnp.cos(ang), jnp.sin(ang)
    one, zero = jnp.ones((L, 64), F32), jnp.zeros((L, 64), F32)
    cos2 = jnp.concatenate([cos, cos, cos, cos], axis=1)
    sin2 = jnp.concatenate([-sin, sin, -sin, sin], axis=1)
    cosk = jnp.concatenate([cos, cos, one], axis=1)
    sink = jnp.concatenate([-sin, sin, zero], axis=1)
    return cos2, sin2, cosk, sink


def _pick_tile(L, want):
    t = min(want, L)
    while L % t:
        t //= 2
    return t


def kernel(x, mix_norm, mlp_norm, w_ff1, w_ff2, ev_w_in, ev_kv_norm, ev_w_uk, ev_w_uv, ev_pool_w, ev_pool_scale, ev_w_out, od_w_in, od_conv_w, od_a_log, od_dt_bias, od_o_norm, od_w_out, final_norm):
    B, L, D = x.shape
    T = B * L
    depth = mix_norm.shape[0]
    mm = MXU_DTYPE
    h = x.reshape(T, D)
    tabs = _rope_tables(L)
    tm = _pick_tile(L, 512)
    tm_mlp = _pick_tile(L, 512)
    tq = _pick_tile(L, 128)
    kc = _pick_tile(L, 512)
    gdn_rows = _pick_tile(L, 512)
    fg = final_norm.reshape(1, D)

    for layer in range(depth):
        j = layer // 2
        g_mix = mix_norm[layer].reshape(1, D)
        if layer % 2 == 0:
            w = ev_w_in[j]
            c0 = A_WIDTH
            c1 = c0 + A_KV_RANK
            c2 = c1 + IDX_HEADS * IDX_DIM
            c3 = c2 + IDX_DIM
            c4 = c3 + IDX_HEADS
            pad = jnp.zeros((D, LANES - IDX_DIM - IDX_HEADS), w.dtype)
            w_all = jnp.concatenate([w[:, :c0], w[:, c1:c2], w[:, c4:], w[:, c0:c1], w[:, c2:c3], w[:, c3:c4], pad], axis=1).astype(mm)
            wkv = jnp.concatenate([ev_w_uk[j], ev_w_uv[j]], axis=1).astype(mm)
            qt, qit, u, kv, vt, kib, wit = _even_proj(h, g_mix, w_all, ev_kv_norm[j].reshape(1, -1), wkv, tabs, L, tm, tq, kc)
            ya = _dsa(qt, qit, wit, kv, vt, kib, B, L, tq, kc)
            y = _pool(ya, u, ev_pool_w[j].astype(mm), ev_pool_scale[j].reshape(1, -1), L, tm)
            w_out = ev_w_out[j].astype(mm)
        else:
            w = od_w_in[j]
            pad = jnp.zeros((D, LANES - 2 * C_HEADS), w.dtype)
            w_all = jnp.concatenate([w, pad], axis=1).astype(mm)
            qkv, z, ba = _odd_proj(h, g_mix, w_all, tm)
            zero8 = jnp.zeros((C_HEADS,), F32)
            lane_pad = jnp.zeros((LANES - 2 * C_HEADS,), F32)
            alog_row = jnp.concatenate([zero8, od_a_log[j], lane_pad]).reshape(1, LANES)
            dtb_row = jnp.concatenate([zero8, od_dt_bias[j], lane_pad]).reshape(1, LANES)
            gate, cum = _gates(ba, alog_row, dtb_row, tm)
            beta = gate[:, :C_HEADS]
            gcum = cum[:, C_HEADS:2 * C_HEADS]
            gcol = jnp.stack([beta.T, gcum.T], axis=-1)
            grow = jnp.stack([beta.reshape(B, L, C_HEADS), gcum.reshape(B, L, C_HEADS)], axis=1)
            grow = jnp.transpose(grow, (0, 3, 1, 2))
            kcd, qd, kdt, attn, val = _gdn_prep(qkv, od_conv_w[j], gcol, grow, B, L, gdn_rows)
            y = _gdn_scan(kcd, qd, kdt, attn, val, grow, z, od_o_norm[j].reshape(1, -1), B, L, gdn_rows)
            w_out = od_w_out[j].astype(mm)
        h = _out_mlp(h, y, w_out, mlp_norm[layer].reshape(1, D), w_ff1[layer].astype(mm), w_ff2[layer].astype(mm),
                     fg, layer == depth - 1, tm_mlp)
    return h.reshape(B, L, D)
```

```python
import functools

import numpy as np
import jax
import jax.numpy as jnp
from jax import lax
from jax.experimental import pallas as pl
from jax.experimental.pallas import tpu as pltpu

F32 = jnp.float32
I32 = jnp.int32
MXU_DTYPE = jnp.bfloat16

A_HEADS = 8
A_HEAD_DIM = 64
A_WIDTH = A_HEADS * A_HEAD_DIM
A_KV_RANK = 128
IDX_HEADS = 8
IDX_DIM = 64
TOPK_MAX = 256
POOL_WINDOWS = (2, 4, 8, 16)
POOL_GROUP = 128
POOL_WIDTH = POOL_GROUP * len(POOL_WINDOWS)
C_HEADS = 8
C_HEAD_DIM = 128
C_WIDTH = C_HEADS * C_HEAD_DIM
CONV_WIDTH = 4
CHUNK = 128
ROPE_THETA = 10000.0
NORM_EPS = 1e-6

LANES = 128
INT_MIN = np.int32(-2 ** 31)
ABOVE = np.int32(2 ** 31 - 1)
BELOW = np.int32(2 ** 31 - 2)
MASKED_LOGIT = -1e30
COUNT_ROWS = 32
LOG2_E = float(np.log2(np.e))

NT_DIMS = (((1,), (1,)), ((), ()))


def _params(semantics, vmem_mb):
    return pltpu.CompilerParams(dimension_semantics=semantics, vmem_limit_bytes=vmem_mb * 1024 * 1024)


def _dot(a, b):
    return jnp.dot(a, b, preferred_element_type=F32)


def _dot_nt(a, b):
    return lax.dot_general(a, b, NT_DIMS, preferred_element_type=F32)


def _split(a):
    hi = a.astype(MXU_DTYPE)
    lo = (a - hi.astype(F32)).astype(MXU_DTYPE)
    return hi, lo


def _dot3(a, b):
    ah, al = _split(a)
    bh, bl = _split(b)
    return _dot(ah, bh) + (_dot(ah, bl) + _dot(al, bh))


def _rms(x, g):
    return x * lax.rsqrt(jnp.mean(x * x, axis=-1, keepdims=True) + NORM_EPS) * g


def _silu(x):
    return x * jax.nn.sigmoid(x)


def _resident(shape):
    nd = len(shape)
    return pl.BlockSpec(shape, lambda *_: (0,) * nd, pipeline_mode=pl.Buffered(1))


def _rope(x, cos, sin_signed, first_half):
    partner = jnp.where(first_half, pltpu.roll(x, LANES - 32, 1), pltpu.roll(x, 32, 1))
    return x * cos + partner * sin_signed


def _even_proj_kernel(h_ref, g_ref, w_ref, kvn_ref, wkv_ref, cos_ref, sin_ref, cosk_ref, sink_ref,
                      qt_ref, qit_ref, u_ref, kv_ref, vt_ref, kib_ref, wit_ref, *, tq, kc):
    tm = h_ref.shape[0]
    xn = _rms(h_ref[...], g_ref[...]).astype(MXU_DTYPE)
    p = _dot(xn, w_ref[...])
    lane = lax.broadcasted_iota(I32, (tm, LANES), 1)
    first_half = (lane % 64) < 32
    cos, sin = cos_ref[...], sin_ref[...]
    cosk, sink = cosk_ref[...], sink_ref[...]

    def heads_out(o_ref, base, scale):
        for s in range(4):
            r = _rope(p[:, base + s * LANES: base + (s + 1) * LANES], cos, sin, first_half)
            if scale != 1.0:
                r = r * scale
            for j in range(tm // tq):
                rt = r[j * tq:(j + 1) * tq, :].T.astype(o_ref.dtype)
                o_ref[j, :, 2 * s * tq:(2 * s + 1) * tq] = rt[:64, :]
                o_ref[j, :, (2 * s + 1) * tq:(2 * s + 2) * tq] = rt[64:, :]

    heads_out(qt_ref, 0, (A_HEAD_DIM ** -0.5) * LOG2_E)
    heads_out(qit_ref, A_WIDTH, 1.0)
    u_ref[...] = p[:, 1024:1536]
    ckv = _rms(p[:, 1536:1664], kvn_ref[...]).astype(MXU_DTYPE)
    kv = _rope(_dot(ckv, wkv_ref[...]), cosk, sink, first_half)
    kv_ref[...] = kv.astype(kv_ref.dtype)
    for j in range(tm // kc):
        vt_ref[j] = kv[j * kc:(j + 1) * kc, :].T[64:, :].astype(vt_ref.dtype)
    kiw = _rope(p[:, 1664:1792], cosk, sink, first_half)
    kib_ref[...] = kiw.astype(kib_ref.dtype)
    wit = kiw.T[64:64 + IDX_HEADS, :]
    wit_ref[...] = (wit * (IDX_HEADS ** -0.5)) * (IDX_DIM ** -0.5)


def _even_proj(h2, g, w_all, kvn, wkv, tabs, L, tm, tq, kc):
    T, D = h2.shape
    n_pos = L // tm
    N = w_all.shape[1]
    row = lambda i: (i, 0)
    pos = lambda i: (i % n_pos, 0)
    tab_spec = pl.BlockSpec((tm, LANES), pos)
    head_spec = pl.BlockSpec((tm // tq, 64, A_HEADS * tq), lambda i: (i, 0, 0))
    head_shape = jax.ShapeDtypeStruct((T // tq, 64, A_HEADS * tq), MXU_DTYPE)
    return pl.pallas_call(
        functools.partial(_even_proj_kernel, tq=tq, kc=kc),
        grid=(T // tm,),
        in_specs=[pl.BlockSpec((tm, D), row), _resident((1, D)), _resident((D, N)), _resident((1, A_KV_RANK)),
                  _resident((A_KV_RANK, LANES)), tab_spec, tab_spec, tab_spec, tab_spec],
        out_specs=[head_spec, head_spec, pl.BlockSpec((tm, POOL_WIDTH), row), pl.BlockSpec((tm, LANES), row),
                   pl.BlockSpec((tm // kc, 64, kc), lambda i: (i, 0, 0)), pl.BlockSpec((tm, LANES), row),
                   pl.BlockSpec((IDX_HEADS, tm), lambda i: (0, i))],
        out_shape=[head_shape, head_shape, jax.ShapeDtypeStruct((T, POOL_WIDTH), F32),
                   jax.ShapeDtypeStruct((T, LANES), MXU_DTYPE), jax.ShapeDtypeStruct((T // kc, 64, kc), MXU_DTYPE),
                   jax.ShapeDtypeStruct((T, LANES), MXU_DTYPE), jax.ShapeDtypeStruct((IDX_HEADS, T), F32)],
        compiler_params=_params(("parallel",), 40),
        name="even_proj",
    )(h2, g, w_all, kvn, wkv, *tabs)


def _dsa_kernel(qt_ref, qit_ref, wit_ref, kv_ref, vt_ref, kib_ref, o_ref,
                keys_ref, s_ref, p_ref, m_ref, l_ref, a_ref, acc_ref, *, tq, kc, topk, pos_bits):
    i = pl.program_id(1)
    n_chunks = ((i + 1) * tq + kc - 1) // kc
    nh = A_HEADS
    qpos = i * tq + lax.broadcasted_iota(I32, (1, tq), 1)
    kpos0 = lax.broadcasted_iota(I32, (kc, tq), 0)
    wit = wit_ref[...]

    def score_chunk(c, carry):
        kic = kib_ref[pl.ds(pl.multiple_of(c * kc, kc), kc), :]
        s_ref[...] = _dot(kic[:, :IDX_DIM], qit_ref[0])
        score = jnp.zeros((kc, tq), F32)
        for h in range(nh):
            score = score + jnp.maximum(s_ref[:, h * tq:(h + 1) * tq], 0.0) * wit[h:h + 1, :]
        bits = lax.bitcast_convert_type(score, I32)
        mag = bits & np.int32(0x7FFFFFFF)
        key = jnp.where(bits < 0, -mag, mag)
        keys_ref[c] = jnp.where(c * kc + kpos0 <= qpos, key, INT_MIN)
        return carry

    lax.fori_loop(0, n_chunks, score_chunk, 0)

    def count(pred):
        def body(c, acc):
            w = jnp.where(pred(keys_ref[c], c), 1, 0)
            return acc + jnp.sum(w.reshape(kc // COUNT_ROWS, COUNT_ROWS, tq), axis=0)
        acc = lax.fori_loop(0, n_chunks, body, jnp.zeros((COUNT_ROWS, tq), I32))
        return jnp.sum(acc, axis=0, keepdims=True)

    k_eff = jnp.minimum(qpos + 1, topk)

    def thr_bit(it, thr):
        cand = thr ^ lax.shift_left(np.int32(1), 31 - it)
        cnt = count(lambda kb, c: kb >= cand)
        return jnp.where(cnt >= k_eff, cand, thr)

    thr = lax.fori_loop(0, 32, thr_bit, jnp.full((1, tq), INT_MIN, I32))

    def recode(c, carry):
        n_gt, n_eq = carry
        kb = keys_ref[c]
        gt = kb > thr
        eq = kb == thr
        keys_ref[c] = jnp.where(gt, ABOVE, jnp.where(eq, c * kc + kpos0, BELOW))
        fold = lambda m: jnp.sum(jnp.where(m, 1, 0).reshape(kc // COUNT_ROWS, COUNT_ROWS, tq), axis=0)
        return n_gt + fold(gt), n_eq + fold(eq)

    zero_counts = jnp.zeros((COUNT_ROWS, tq), I32)
    n_gt, n_eq = lax.fori_loop(0, n_chunks, recode, (zero_counts, zero_counts))
    n_gt = jnp.sum(n_gt, axis=0, keepdims=True)
    n_eq = jnp.sum(n_eq, axis=0, keepdims=True)

    need = k_eff - n_gt
    has_tie = jnp.max(n_gt + n_eq - k_eff) > 0

    def cut_bit(it, cut):
        cand = cut | lax.shift_left(np.int32(1), pos_bits - 1 - it)
        cnt = count(lambda kb, c: kb < cand)
        return jnp.where(cnt < need, cand, cut)

    cut = lax.cond(has_tie,
                   lambda: lax.fori_loop(0, pos_bits, cut_bit, jnp.zeros((1, tq), I32)),
                   lambda: jnp.full((1, tq), 2 ** pos_bits - 1, I32))

    m_ref[...] = jnp.full(m_ref.shape, MASKED_LOGIT, F32)
    l_ref[...] = jnp.zeros(l_ref.shape, F32)
    acc_ref[...] = jnp.zeros(acc_ref.shape, F32)

    def attend_chunk(c, carry):
        kvc = kv_ref[pl.ds(pl.multiple_of(c * kc, kc), kc), :][:, :A_HEAD_DIM]
        kb = keys_ref[c]
        sel = jnp.logical_or(kb == ABOVE, kb <= cut)
        cap = jnp.where(sel, jnp.inf, MASKED_LOGIT)
        for h in range(nh):
            cols = slice(h * tq, (h + 1) * tq)
            s = jnp.minimum(_dot(kvc, qt_ref[0, :, cols]), cap)
            s_ref[:, cols] = s
            m_old = m_ref[:, cols]
            m_new = jnp.maximum(m_old, jnp.max(s, axis=0, keepdims=True))
            m_ref[:, cols] = m_new
            a_ref[:, cols] = jnp.exp2(m_old - m_new)
        for h in range(nh):
            cols = slice(h * tq, (h + 1) * tq)
            p = jnp.exp2(s_ref[:, cols] - m_ref[:, cols])
            l_ref[:, cols] = a_ref[:, cols] * l_ref[:, cols] + jnp.sum(p, axis=0, keepdims=True)
            p_ref[:, cols] = p.astype(p_ref.dtype)
        acc_ref[...] = acc_ref[...] * a_ref[...] + _dot(vt_ref[c], p_ref[...])
        return carry

    lax.fori_loop(0, n_chunks, attend_chunk, 0)

    out_t = jnp.concatenate([acc_ref[:, h * tq:(h + 1) * tq] / l_ref[:, h * tq:(h + 1) * tq] for h in range(nh)], axis=0)
    o_ref[...] = out_t.T.astype(o_ref.dtype)


def _dsa(qt, qit, wit, kv, vt, kib, B, L, tq, kc):
    T = B * L
    nq = L // tq
    nkc = L // kc
    topk = min(TOPK_MAX, L // 4)
    pos_bits = max(1, int(np.ceil(np.log2(L))))
    head_spec = pl.BlockSpec((1, 64, A_HEADS * tq), lambda b, i: (b * nq + i, 0, 0))
    seq_spec = pl.BlockSpec((L, LANES), lambda b, i: (b, 0))
    kern = functools.partial(_dsa_kernel, tq=tq, kc=kc, topk=topk, pos_bits=pos_bits)
    return pl.pallas_call(
        kern,
        grid=(B, nq),
        in_specs=[head_spec, head_spec, pl.BlockSpec((IDX_HEADS, tq), lambda b, i: (0, b * nq + i)),
                  seq_spec, pl.BlockSpec((nkc, 64, kc), lambda b, i: (b, 0, 0)), seq_spec],
        out_specs=pl.BlockSpec((tq, A_WIDTH), lambda b, i: (b * nq + i, 0)),
        out_shape=jax.ShapeDtypeStruct((T, A_WIDTH), MXU_DTYPE),
        scratch_shapes=[pltpu.VMEM((nkc, kc, tq), I32), pltpu.VMEM((kc, A_HEADS * tq), F32),
                        pltpu.VMEM((kc, A_HEADS * tq), MXU_DTYPE),
                        pltpu.VMEM((1, A_HEADS * tq), F32), pltpu.VMEM((1, A_HEADS * tq), F32),
                        pltpu.VMEM((1, A_HEADS * tq), F32), pltpu.VMEM((A_HEAD_DIM, A_HEADS * tq), F32)],
        compiler_params=_params(("parallel", "arbitrary"), 48),
        name="dsa",
    )(qt, qit, wit, kv, vt, kib)


POOL_HALO = 16


def _pool_kernel(ya_ref, u_ref, halo_ref, pw_ref, ps_ref, o_ref, *, n_pos):
    tm = u_ref.shape[0]
    i = pl.program_id(0)
    seq_start = (i % n_pos) == 0
    u = u_ref[...]
    halo = jnp.where(seq_start, 0.0, halo_ref[...])
    ue = jnp.concatenate([halo, u], axis=0)
    t = (i % n_pos) * tm + lax.broadcasted_iota(I32, (tm, 1), 0)
    o_ref[:, :A_WIDTH] = ya_ref[...]
    for gi, w in enumerate(POOL_WINDOWS):
        lanes = slice(gi * POOL_GROUP, (gi + 1) * POOL_GROUP)
        s = ue[:, lanes]
        span = 1
        while span < w:
            s = s + pltpu.roll(s, span, 0)
            span *= 2
        count = jnp.minimum(t + 1, w).astype(F32)
        pooled = s[POOL_HALO:, :] / count - u[:, lanes]
        y = _dot(pooled.astype(MXU_DTYPE), pw_ref[gi]) * ps_ref[:, lanes]
        o_ref[:, A_WIDTH + gi * POOL_GROUP: A_WIDTH + (gi + 1) * POOL_GROUP] = y.astype(o_ref.dtype)


def _pool(ya, u, pool_w, pool_scale, L, tm):
    T = u.shape[0]
    n_pos = L // tm
    row = lambda i: (i, 0)
    halo_map = lambda i: (jnp.maximum(i * (tm // POOL_HALO) - 1, 0), 0)
    return pl.pallas_call(
        functools.partial(_pool_kernel, n_pos=n_pos),
        grid=(T // tm,),
        in_specs=[pl.BlockSpec((tm, A_WIDTH), row), pl.BlockSpec((tm, POOL_WIDTH), row),
                  pl.BlockSpec((POOL_HALO, POOL_WIDTH), halo_map),
                  _resident(pool_w.shape), _resident((1, POOL_WIDTH))],
        out_specs=pl.BlockSpec((tm, A_WIDTH + POOL_WIDTH), row),
        out_shape=jax.ShapeDtypeStruct((T, A_WIDTH + POOL_WIDTH), MXU_DTYPE),
        compiler_params=_params(("parallel",), 32),
        name="pool",
    )(ya, u, u, pool_w, pool_scale)


def _odd_proj_kernel(h_ref, g_ref, w_ref, qkv_ref, z_ref, ba_ref):
    xn = _rms(h_ref[...], g_ref[...]).astype(MXU_DTYPE)
    n_qkv = qkv_ref.shape[1]
    n_z = z_ref.shape[1]
    step = 512
    for c in range(n_qkv // step):
        qkv_ref[:, c * step:(c + 1) * step] = _dot(xn, w_ref[:, c * step:(c + 1) * step])
    for c in range(n_z // step):
        z_ref[:, c * step:(c + 1) * step] = _dot(xn, w_ref[:, n_qkv + c * step: n_qkv + (c + 1) * step])
    ba_ref[...] = _dot(xn, w_ref[:, n_qkv + n_z:])


def _odd_proj(h2, g, w_all, tm):
    T, D = h2.shape
    N = w_all.shape[1]
    row = lambda i: (i, 0)
    return pl.pallas_call(
        _odd_proj_kernel,
        grid=(T // tm,),
        in_specs=[pl.BlockSpec((tm, D), row), _resident((1, D)), _resident((D, N))],
        out_specs=[pl.BlockSpec((tm, 3 * C_WIDTH), row), pl.BlockSpec((tm, C_WIDTH), row), pl.BlockSpec((tm, LANES), row)],
        out_shape=[jax.ShapeDtypeStruct((T, 3 * C_WIDTH), F32), jax.ShapeDtypeStruct((T, C_WIDTH), F32),
                   jax.ShapeDtypeStruct((T, LANES), F32)],
        compiler_params=_params(("parallel",), 48),
        name="odd_proj",
    )(h2, g, w_all)


def _gates_kernel(ba_ref, alog_ref, dtb_ref, gate_ref, cum_ref):
    tm = ba_ref.shape[0]
    x = ba_ref[...]
    lane = lax.broadcasted_iota(I32, x.shape, 1)
    beta = jax.nn.sigmoid(x)
    g = -jnp.exp(alog_ref[...]) * jnp.logaddexp(x + dtb_ref[...], 0.0)
    gate = jnp.where(lane < C_HEADS, beta, jnp.where(lane < 2 * C_HEADS, g, 0.0))
    gate_ref[...] = gate
    ii = lax.broadcasted_iota(I32, (CHUNK, CHUNK), 0)
    jj = lax.broadcasted_iota(I32, (CHUNK, CHUNK), 1)
    tri = (ii >= jj).astype(F32)
    for c in range(tm // CHUNK):
        rows = slice(c * CHUNK, (c + 1) * CHUNK)
        cum_ref[rows, :] = _dot3(tri, gate[rows, :])


def _gates(ba, alog_row, dtb_row, tm):
    T = ba.shape[0]
    row = lambda i: (i, 0)
    return pl.pallas_call(
        _gates_kernel,
        grid=(T // tm,),
        in_specs=[pl.BlockSpec((tm, LANES), row), _resident((1, LANES)), _resident((1, LANES))],
        out_specs=[pl.BlockSpec((tm, LANES), row), pl.BlockSpec((tm, LANES), row)],
        out_shape=[jax.ShapeDtypeStruct((T, LANES), F32), jax.ShapeDtypeStruct((T, LANES), F32)],
        compiler_params=_params(("parallel",), 16),
        name="gates",
    )(ba, alog_row, dtb_row)


CONV_HALO = 8


def _gdn_prep_kernel(xq_ref, xk_ref, xv_ref, hq_ref, hk_ref, hv_ref, wq_ref, wk_ref, wv_ref, gcol_ref, grow_ref,
                     kcd_ref, qd_ref, kdt_ref, attn_ref, val_ref):
    rows = xq_ref.shape[0]
    seq_start = pl.program_id(2) == 0

    def conv_silu(x_ref, halo_ref, w_ref):
        x = x_ref[...]
        halo = jnp.where(seq_start, 0.0, halo_ref[...])
        xe = jnp.concatenate([halo, x], axis=0)
        w = w_ref[...]
        y = w[CONV_WIDTH - 1:CONV_WIDTH, :] * x
        for j in range(CONV_WIDTH - 1):
            back = CONV_WIDTH - 1 - j
            y = y + w[j:j + 1, :] * pltpu.roll(xe, back, 0)[CONV_HALO:, :]
        return _silu(y)

    def l2n(x):
        return x * lax.rsqrt(jnp.sum(x * x, axis=-1, keepdims=True) + NORM_EPS)

    q = l2n(conv_silu(xq_ref, hq_ref, wq_ref)) * (C_HEAD_DIM ** -0.5)
    k = l2n(conv_silu(xk_ref, hk_ref, wk_ref))
    v = conv_silu(xv_ref, hv_ref, wv_ref)
    gcol = gcol_ref[0]
    grow = grow_ref[0, 0]

    ii = lax.broadcasted_iota(I32, (CHUNK, CHUNK), 0)
    jj = lax.broadcasted_iota(I32, (CHUNK, CHUNK), 1)
    tril = ii >= jj
    strict = ii > jj
    eye = (ii == jj).astype(F32)

    sls = [slice(c * CHUNK, (c + 1) * CHUNK) for c in range(rows // CHUNK)]
    beta = [gcol[s, 0:1] for s in sls]
    gc = [gcol[s, 1:2] for s in sls]
    decay = [jnp.exp(jnp.where(tril, g - grow[1:2, s], -jnp.inf)) for g, s in zip(gc, sls)]
    kb = [k[s].astype(MXU_DTYPE) for s in sls]
    kk = [_dot_nt(x, x) for x in kb]
    lmat = [jnp.where(strict, b * x * d, 0.0) for b, x, d in zip(beta, kk, decay)]
    tmat = [eye - jnp.where((ii >> 1) == (jj >> 1), x, 0.0) for x in lmat]
    shift = 1
    while (2 << shift) <= CHUNK:
        couple = jnp.logical_and((ii >> (shift + 1)) == (jj >> (shift + 1)), (ii >> shift) != (jj >> shift))
        cd = [_dot(jnp.where(couple, x, 0.0).astype(MXU_DTYPE), t.astype(MXU_DTYPE)) for x, t in zip(lmat, tmat)]
        tmat = [t - _dot(t.astype(MXU_DTYPE), y.astype(MXU_DTYPE)) for t, y in zip(tmat, cd)]
        shift += 1
    eg = [jnp.exp(g) for g in gc]
    for c, s in enumerate(sls):
        rhs = jnp.concatenate([v[s] * beta[c], k[s] * (beta[c] * eg[c])], axis=1).astype(MXU_DTYPE)
        solved = _dot(tmat[c].astype(MXU_DTYPE), rhs)
        val_ref[s, :] = solved[:, :C_HEAD_DIM]
        kcd_ref[s, :] = solved[:, C_HEAD_DIM:].astype(kcd_ref.dtype)
        attn_ref[s, :] = (_dot_nt(q[s].astype(MXU_DTYPE), kb[c]) * decay[c]).astype(attn_ref.dtype)
        qd_ref[s, :] = (q[s] * eg[c]).astype(qd_ref.dtype)
        g_last = gc[c][CHUNK - 1:CHUNK, :]
        kdt_ref[s, :] = (k[s] * jnp.exp(g_last - gc[c])).T.astype(kdt_ref.dtype)


def _gdn_prep(qkv, conv_w, gcol, grow, B, L, rows):
    T = B * L
    H = C_HEADS
    nr = L // rows
    hb = rows // CONV_HALO

    def x_spec(part):
        return pl.BlockSpec((rows, C_HEAD_DIM), lambda b, h, r: (b * nr + r, part * H + h))

    def halo_spec(part):
        return pl.BlockSpec((CONV_HALO, C_HEAD_DIM), lambda b, h, r: (jnp.maximum((b * nr + r) * hb - 1, 0), part * H + h))

    def w_spec(part):
        return pl.BlockSpec((CONV_WIDTH, C_HEAD_DIM), lambda b, h, r: (0, part * H + h))

    out_spec = pl.BlockSpec((rows, C_HEAD_DIM), lambda b, h, r: (b * nr + r, h))
    mm_shape = jax.ShapeDtypeStruct((T, C_WIDTH), MXU_DTYPE)
    return pl.pallas_call(
        _gdn_prep_kernel,
        grid=(B, H, nr),
        in_specs=[x_spec(0), x_spec(1), x_spec(2), halo_spec(0), halo_spec(1), halo_spec(2),
                  w_spec(0), w_spec(1), w_spec(2),
                  pl.BlockSpec((1, rows, 2), lambda b, h, r: (h, b * nr + r, 0)),
                  pl.BlockSpec((1, 1, 2, rows), lambda b, h, r: (b, h, 0, r))],
        out_specs=[out_spec] * 5,
        out_shape=[mm_shape, mm_shape, mm_shape, mm_shape, jax.ShapeDtypeStruct((T, C_WIDTH), F32)],
        compiler_params=_params(("parallel", "parallel", "parallel"), 32),
        name="gdn_prep",
    )(qkv, qkv, qkv, qkv, qkv, qkv, conv_w, conv_w, conv_w, gcol, grow)


def _gdn_scan_kernel(kcd_ref, qd_ref, kdt_ref, attn_ref, val_ref, grow_ref, z_ref, on_ref, y_ref, state_ref):
    rows = kcd_ref.shape[0]
    nh = C_HEADS

    @pl.when(pl.program_id(1) == 0)
    def _():
        state_ref[...] = jnp.zeros(state_ref.shape, F32)

    on = on_ref[...]
    for c in range(rows // CHUNK):
        s = slice(c * CHUNK, (c + 1) * CHUNK)
        cols = [slice(h * C_HEAD_DIM, (h + 1) * C_HEAD_DIM) for h in range(nh)]
        state = [state_ref[h] for h in range(nh)]
        both = [_dot(jnp.concatenate([kcd_ref[s, cl], qd_ref[s, cl]], axis=0), st.astype(MXU_DTYPE))
                for cl, st in zip(cols, state)]
        v_new = [(val_ref[s, cl] - b[:CHUNK]).astype(MXU_DTYPE) for cl, b in zip(cols, both)]
        res = [_dot(jnp.concatenate([attn_ref[s, cl], kdt_ref[s, cl]], axis=0), vn) for cl, vn in zip(cols, v_new)]
        for h in range(nh):
            g_last = grow_ref[0, h, 1:2, (c + 1) * CHUNK - 1:(c + 1) * CHUNK]
            state_ref[h] = state[h] * jnp.exp(g_last) + res[h][CHUNK:]
            o = both[h][CHUNK:] + res[h][:CHUNK]
            y = _rms(o, on) * _silu(z_ref[s, cols[h]])
            y_ref[s, cols[h]] = y.astype(y_ref.dtype)


def _gdn_scan(kcd, qd, kdt, attn, val, grow, z, o_norm, B, L, rows):
    T = B * L
    nr = L // rows
    row = pl.BlockSpec((rows, C_WIDTH), lambda b, r: (b * nr + r, 0))
    return pl.pallas_call(
        _gdn_scan_kernel,
        grid=(B, nr),
        in_specs=[row, row, row, row, row,
                  pl.BlockSpec((1, C_HEADS, 2, rows), lambda b, r: (b, 0, 0, r)),
                  row, pl.BlockSpec((1, C_HEAD_DIM), lambda b, r: (0, 0))],
        out_specs=row,
        out_shape=jax.ShapeDtypeStruct((T, C_WIDTH), MXU_DTYPE),
        scratch_shapes=[pltpu.VMEM((C_HEADS, C_HEAD_DIM, C_HEAD_DIM), F32)],
        compiler_params=_params(("parallel", "arbitrary"), 48),
        name="gdn_scan",
    )(kcd, qd, kdt, attn, val, grow, z, o_norm)


def _out_mlp_kernel(h_ref, y_ref, wo_ref, g_ref, w1_ref, w2_ref, fg_ref, o_ref, *, ff_chunk, final):
    h1 = h_ref[...] + _dot(y_ref[...], wo_ref[...])
    xn = _rms(h1, g_ref[...]).astype(MXU_DTYPE)
    acc = h1
    for c in range(w1_ref.shape[1] // ff_chunk):
        cols = slice(c * ff_chunk, (c + 1) * ff_chunk)
        a = jnp.square(jnp.maximum(_dot(xn, w1_ref[:, cols]), 0.0)).astype(MXU_DTYPE)
        acc = acc + _dot(a, w2_ref[cols, :])
    if final:
        acc = _rms(acc, fg_ref[...])
    o_ref[...] = acc


def _out_mlp(h2, y, w_out, g, w1, w2, final_g, final, tm):
    T, D = h2.shape
    dff = w1.shape[1]
    row = lambda i: (i, 0)
    return pl.pallas_call(
        functools.partial(_out_mlp_kernel, ff_chunk=512, final=final),
        grid=(T // tm,),
        in_specs=[pl.BlockSpec((tm, D), row), pl.BlockSpec((tm, y.shape[1]), row), _resident(w_out.shape),
                  _resident((1, D)), _resident((D, dff)), _resident((dff, D)), _resident((1, D))],
        out_specs=pl.BlockSpec((tm, D), row),
        out_shape=jax.ShapeDtypeStruct((T, D), F32),
        compiler_params=_params(("parallel",), 48),
        name="out_mlp",
    )(h2, y, w_out, g, w1, w2, final_g)


def _rope_tables(L):
    inv = ROPE_THETA ** (-jnp.arange(0, A_HEAD_DIM, 2, dtype=F32) / A_HEAD_DIM)
    ang = jnp.arange(L, dtype=F32)[:, None] * inv[None, :]
    cos, sin = jnp.cos(ang), jnp.sin(ang)
    one, zero = jnp.ones((L, 64), F32), jnp.zeros((L, 64), F32)
    cos2 = jnp.concatenate([cos, cos, cos, cos], axis=1)
    sin2 = jnp.concatenate([-sin, sin, -sin, sin], axis=1)
    cosk = jnp.concatenate([cos, cos, one], axis=1)
    sink = jnp.concatenate([-sin, sin, zero], axis=1)
    return cos2, sin2, cosk, sink


def _pick_tile(L, want):
    t = min(want, L)
    while L % t:
        t //= 2
    return t


def kernel(x, mix_norm, mlp_norm, w_ff1, w_ff2, ev_w_in, ev_kv_norm, ev_w_uk, ev_w_uv, ev_pool_w, ev_pool_scale, ev_w_out, od_w_in, od_conv_w, od_a_log, od_dt_bias, od_o_norm, od_w_out, final_norm):
    B, L, D = x.shape
    T = B * L
    depth = mix_norm.shape[0]
    mm = MXU_DTYPE
    h = x.reshape(T, D)
    tabs = _rope_tables(L)
    tm = _pick_tile(L, 512)
    tm_mlp = _pick_tile(L, 512)
    tq = _pick_tile(L, 256)
    kc = _pick_tile(L, 512)
    gdn_rows = _pick_tile(L, 512)
    fg = final_norm.reshape(1, D)

    for layer in range(depth):
        j = layer // 2
        g_mix = mix_norm[layer].reshape(1, D)
        if layer % 2 == 0:
            w = ev_w_in[j]
            c0 = A_WIDTH
            c1 = c0 + A_KV_RANK
            c2 = c1 + IDX_HEADS * IDX_DIM
            c3 = c2 + IDX_DIM
            c4 = c3 + IDX_HEADS
            pad = jnp.zeros((D, LANES - IDX_DIM - IDX_HEADS), w.dtype)
            w_all = jnp.concatenate([w[:, :c0], w[:, c1:c2], w[:, c4:], w[:, c0:c1], w[:, c2:c3], w[:, c3:c4], pad], axis=1).astype(mm)
            wkv = jnp.concatenate([ev_w_uk[j], ev_w_uv[j]], axis=1).astype(mm)
            qt, qit, u, kv, vt, kib, wit = _even_proj(h, g_mix, w_all, ev_kv_norm[j].reshape(1, -1), wkv, tabs, L, tm, tq, kc)
            ya = _dsa(qt, qit, wit, kv, vt, kib, B, L, tq, kc)
            y = _pool(ya, u, ev_pool_w[j].astype(mm), ev_pool_scale[j].reshape(1, -1), L, tm)
            w_out = ev_w_out[j].astype(mm)
        else:
            w = od_w_in[j]
            pad = jnp.zeros((D, LANES - 2 * C_HEADS), w.dtype)
            w_all = jnp.concatenate([w, pad], axis=1).astype(mm)
            qkv, z, ba = _odd_proj(h, g_mix, w_all, tm)
            zero8 = jnp.zeros((C_HEADS,), F32)
            lane_pad = jnp.zeros((LANES - 2 * C_HEADS,), F32)
            alog_row = jnp.concatenate([zero8, od_a_log[j], lane_pad]).reshape(1, LANES)
            dtb_row = jnp.concatenate([zero8, od_dt_bias[j], lane_pad]).reshape(1, LANES)
            gate, cum = _gates(ba, alog_row, dtb_row, tm)
            beta = gate[:, :C_HEADS]
            gcum = cum[:, C_HEADS:2 * C_HEADS]
            gcol = jnp.stack([beta.T, gcum.T], axis=-1)
            grow = jnp.stack([beta.reshape(B, L, C_HEADS), gcum.reshape(B, L, C_HEADS)], axis=1)
            grow = jnp.transpose(grow, (0, 3, 1, 2))
            kcd, qd, kdt, attn, val = _gdn_prep(qkv, od_conv_w[j], gcol, grow, B, L, gdn_rows)
            y = _gdn_scan(kcd, qd, kdt, attn, val, grow, z, od_o_norm[j].reshape(1, -1), B, L, gdn_rows)
            w_out = od_w_out[j].astype(mm)
        h = _out_mlp(h, y, w_out, mlp_norm[layer].reshape(1, D), w_ff1[layer].astype(mm), w_ff2[layer].astype(mm),
                     fg, layer == depth - 1, tm_mlp)
    return h.reshape(B, L, D)
```

```python
import functools

import numpy as np
import jax
import jax.numpy as jnp
from jax import lax
from jax.experimental import pallas as pl
from jax.experimental.pallas import tpu as pltpu

F32 = jnp.float32
I32 = jnp.int32
MXU_DTYPE = jnp.bfloat16

A_HEADS = 8
A_HEAD_DIM = 64
A_WIDTH = A_HEADS * A_HEAD_DIM
A_KV_RANK = 128
IDX_HEADS = 8
IDX_DIM = 64
TOPK_MAX = 256
POOL_WINDOWS = (2, 4, 8, 16)
POOL_GROUP = 128
POOL_WIDTH = POOL_GROUP * len(POOL_WINDOWS)
C_HEADS = 8
C_HEAD_DIM = 128
C_WIDTH = C_HEADS * C_HEAD_DIM
CONV_WIDTH = 4
CHUNK = 128
ROPE_THETA = 10000.0
NORM_EPS = 1e-6

LANES = 128
INT_MIN = np.int32(-2 ** 31)
ABOVE = np.int32(2 ** 31 - 1)
BELOW = np.int32(2 ** 31 - 2)
MASKED_LOGIT = -1e30
COUNT_ROWS = 32
LOG2_E = float(np.log2(np.e))

NT_DIMS = (((1,), (1,)), ((), ()))


def _params(semantics, vmem_mb):
    return pltpu.CompilerParams(dimension_semantics=semantics, vmem_limit_bytes=vmem_mb * 1024 * 1024)


def _dot(a, b):
    return jnp.dot(a, b, preferred_element_type=F32)


def _dot_nt(a, b):
    return lax.dot_general(a, b, NT_DIMS, preferred_element_type=F32)


def _split(a):
    hi = a.astype(MXU_DTYPE)
    lo = (a - hi.astype(F32)).astype(MXU_DTYPE)
    return hi, lo


def _dot3(a, b):
    ah, al = _split(a)
    bh, bl = _split(b)
    return _dot(ah, bh) + (_dot(ah, bl) + _dot(al, bh))


def _rms(x, g):
    return x * lax.rsqrt(jnp.mean(x * x, axis=-1, keepdims=True) + NORM_EPS) * g


def _silu(x):
    return x * jax.nn.sigmoid(x)


def _resident(shape):
    nd = len(shape)
    return pl.BlockSpec(shape, lambda *_: (0,) * nd, pipeline_mode=pl.Buffered(1))


def _rope(x, cos, sin_signed, first_half):
    partner = jnp.where(first_half, pltpu.roll(x, LANES - 32, 1), pltpu.roll(x, 32, 1))
    return x * cos + partner * sin_signed


def _even_proj_kernel(h_ref, g_ref, w_ref, kvn_ref, wkv_ref, cos_ref, sin_ref, cosk_ref, sink_ref,
                      qt_ref, qit_ref, u_ref, kv_ref, vt_ref, kib_ref, wit_ref, *, tq, kc):
    tm = h_ref.shape[0]
    xn = _rms(h_ref[...], g_ref[...]).astype(MXU_DTYPE)
    p = _dot(xn, w_ref[...])
    lane = lax.broadcasted_iota(I32, (tm, LANES), 1)
    first_half = (lane % 64) < 32
    cos, sin = cos_ref[...], sin_ref[...]
    cosk, sink = cosk_ref[...], sink_ref[...]

    def heads_out(o_ref, base, scale):
        for s in range(4):
            r = _rope(p[:, base + s * LANES: base + (s + 1) * LANES], cos, sin, first_half)
            if scale != 1.0:
                r = r * scale
            for j in range(tm // tq):
                rt = r[j * tq:(j + 1) * tq, :].T.astype(o_ref.dtype)
                o_ref[j, :, 2 * s * tq:(2 * s + 1) * tq] = rt[:64, :]
                o_ref[j, :, (2 * s + 1) * tq:(2 * s + 2) * tq] = rt[64:, :]

    heads_out(qt_ref, 0, (A_HEAD_DIM ** -0.5) * LOG2_E)
    heads_out(qit_ref, A_WIDTH, 1.0)
    u_ref[...] = p[:, 1024:1536]
    ckv = _rms(p[:, 1536:1664], kvn_ref[...]).astype(MXU_DTYPE)
    kv = _rope(_dot(ckv, wkv_ref[...]), cosk, sink, first_half)
    kv_ref[...] = kv.astype(kv_ref.dtype)
    for j in range(tm // kc):
        vt_ref[j] = kv[j * kc:(j + 1) * kc, :].T[64:, :].astype(vt_ref.dtype)
    kiw = _rope(p[:, 1664:1792], cosk, sink, first_half)
    kib_ref[...] = kiw.astype(kib_ref.dtype)
    wit = kiw.T[64:64 + IDX_HEADS, :]
    wit_ref[...] = (wit * (IDX_HEADS ** -0.5)) * (IDX_DIM ** -0.5)


def _even_proj(h2, g, w_all, kvn, wkv, tabs, L, tm, tq, kc):
    T, D = h2.shape
    n_pos = L // tm
    N = w_all.shape[1]
    row = lambda i: (i, 0)
    pos = lambda i: (i % n_pos, 0)
    tab_spec = pl.BlockSpec((tm, LANES), pos)
    head_spec = pl.BlockSpec((tm // tq, 64, A_HEADS * tq), lambda i: (i, 0, 0))
    head_shape = jax.ShapeDtypeStruct((T // tq, 64, A_HEADS * tq), MXU_DTYPE)
    return pl.pallas_call(
        functools.partial(_even_proj_kernel, tq=tq, kc=kc),
        grid=(T // tm,),
        in_specs=[pl.BlockSpec((tm, D), row), _resident((1, D)), _resident((D, N)), _resident((1, A_KV_RANK)),
                  _resident((A_KV_RANK, LANES)), tab_spec, tab_spec, tab_spec, tab_spec],
        out_specs=[head_spec, head_spec, pl.BlockSpec((tm, POOL_WIDTH), row), pl.BlockSpec((tm, LANES), row),
                   pl.BlockSpec((tm // kc, 64, kc), lambda i: (i, 0, 0)), pl.BlockSpec((tm, LANES), row),
                   pl.BlockSpec((IDX_HEADS, tm), lambda i: (0, i))],
        out_shape=[head_shape, head_shape, jax.ShapeDtypeStruct((T, POOL_WIDTH), F32),
                   jax.ShapeDtypeStruct((T, LANES), MXU_DTYPE), jax.ShapeDtypeStruct((T // kc, 64, kc), MXU_DTYPE),
                   jax.ShapeDtypeStruct((T, LANES), MXU_DTYPE), jax.ShapeDtypeStruct((IDX_HEADS, T), F32)],
        compiler_params=_params(("parallel",), 40),
        name="even_proj",
    )(h2, g, w_all, kvn, wkv, *tabs)


def _dsa_kernel(qt_ref, qit_ref, wit_ref, kv_ref, vt_ref, kib_ref, o_ref,
                keys_ref, s_ref, p_ref, m_ref, l_ref, a_ref, acc_ref, *, tq, kc, topk, pos_bits):
    i = pl.program_id(1)
    n_chunks = ((i + 1) * tq + kc - 1) // kc
    nh = A_HEADS
    qpos = i * tq + lax.broadcasted_iota(I32, (1, tq), 1)
    kpos0 = lax.broadcasted_iota(I32, (kc, tq), 0)
    wit = wit_ref[...]

    def score_chunk(c, carry):
        kic = kib_ref[pl.ds(pl.multiple_of(c * kc, kc), kc), :]
        s_ref[...] = _dot(kic[:, :IDX_DIM], qit_ref[0])
        score = jnp.zeros((kc, tq), F32)
        for h in range(nh):
            score = score + jnp.maximum(s_ref[:, h * tq:(h + 1) * tq], 0.0) * wit[h:h + 1, :]
        bits = lax.bitcast_convert_type(score, I32)
        mag = bits & np.int32(0x7FFFFFFF)
        key = jnp.where(bits < 0, -mag, mag)
        keys_ref[c] = jnp.where(c * kc + kpos0 <= qpos, key, INT_MIN)
        return carry

    lax.fori_loop(0, n_chunks, score_chunk, 0)

    def count(pred):
        def body(c, acc):
            w = jnp.where(pred(keys_ref[c], c), 1, 0)
            return acc + jnp.sum(w.reshape(kc // COUNT_ROWS, COUNT_ROWS, tq), axis=0)
        acc = lax.fori_loop(0, n_chunks, body, jnp.zeros((COUNT_ROWS, tq), I32))
        return jnp.sum(acc, axis=0, keepdims=True)

    k_eff = jnp.minimum(qpos + 1, topk)

    def thr_bit(it, thr):
        cand = thr ^ lax.shift_left(np.int32(1), 31 - it)
        cnt = count(lambda kb, c: kb >= cand)
        return jnp.where(cnt >= k_eff, cand, thr)

    thr = lax.fori_loop(0, 32, thr_bit, jnp.full((1, tq), INT_MIN, I32))

    def recode(c, carry):
        n_gt, n_eq = carry
        kb = keys_ref[c]
        gt = kb > thr
        eq = kb == thr
        keys_ref[c] = jnp.where(gt, ABOVE, jnp.where(eq, c * kc + kpos0, BELOW))
        fold = lambda m: jnp.sum(jnp.where(m, 1, 0).reshape(kc // COUNT_ROWS, COUNT_ROWS, tq), axis=0)
        return n_gt + fold(gt), n_eq + fold(eq)

    zero_counts = jnp.zeros((COUNT_ROWS, tq), I32)
    n_gt, n_eq = lax.fori_loop(0, n_chunks, recode, (zero_counts, zero_counts))
    n_gt = jnp.sum(n_gt, axis=0, keepdims=True)
    n_eq = jnp.sum(n_eq, axis=0, keepdims=True)

    need = k_eff - n_gt
    has_tie = jnp.max(n_gt + n_eq - k_eff) > 0

    def cut_bit(it, cut):
        cand = cut | lax.shift_left(np.int32(1), pos_bits - 1 - it)
        cnt = count(lambda kb, c: kb < cand)
        return jnp.where(cnt < need, cand, cut)

    cut = lax.cond(has_tie,
                   lambda: lax.fori_loop(0, pos_bits, cut_bit, jnp.zeros((1, tq), I32)),
                   lambda: jnp.full((1, tq), 2 ** pos_bits - 1, I32))

    m_ref[...] = jnp.full(m_ref.shape, MASKED_LOGIT, F32)
    l_ref[...] = jnp.zeros(l_ref.shape, F32)
    acc_ref[...] = jnp.zeros(acc_ref.shape, F32)

    def attend_chunk(c, carry):
        kvc = kv_ref[pl.ds(pl.multiple_of(c * kc, kc), kc), :][:, :A_HEAD_DIM]
        kb = keys_ref[c]
        sel = jnp.logical_or(kb == ABOVE, kb <= cut)
        cap = jnp.where(sel, jnp.inf, MASKED_LOGIT)
        for h in range(nh):
            cols = slice(h * tq, (h + 1) * tq)
            s = jnp.minimum(_dot(kvc, qt_ref[0, :, cols]), cap)
            s_ref[:, cols] = s
            m_old = m_ref[:, cols]
            m_new = jnp.maximum(m_old, jnp.max(s, axis=0, keepdims=True))
            m_ref[:, cols] = m_new
            a_ref[:, cols] = jnp.exp2(m_old - m_new)
        for h in range(nh):
            cols = slice(h * tq, (h + 1) * tq)
            p = jnp.exp2(s_ref[:, cols] - m_ref[:, cols])
            l_ref[:, cols] = a_ref[:, cols] * l_ref[:, cols] + jnp.sum(p, axis=0, keepdims=True)
            p_ref[:, cols] = p.astype(p_ref.dtype)
        acc_ref[...] = acc_ref[...] * a_ref[...] + _dot(vt_ref[c], p_ref[...])
        return carry

    lax.fori_loop(0, n_chunks, attend_chunk, 0)

    out_t = jnp.concatenate([acc_ref[:, h * tq:(h + 1) * tq] / l_ref[:, h * tq:(h + 1) * tq] for h in range(nh)], axis=0)
    o_ref[...] = out_t.T.astype(o_ref.dtype)


def _dsa(qt, qit, wit, kv, vt, kib, B, L, tq, kc):
    T = B * L
    nq = L // tq
    nkc = L // kc
    topk = min(TOPK_MAX, L // 4)
    pos_bits = max(1, int(np.ceil(np.log2(L))))
    head_spec = pl.BlockSpec((1, 64, A_HEADS * tq), lambda b, i: (b * nq + i, 0, 0))
    seq_spec = pl.BlockSpec((L, LANES), lambda b, i: (b, 0))
    kern = functools.partial(_dsa_kernel, tq=tq, kc=kc, topk=topk, pos_bits=pos_bits)
    return pl.pallas_call(
        kern,
        grid=(B, nq),
        in_specs=[head_spec, head_spec, pl.BlockSpec((IDX_HEADS, tq), lambda b, i: (0, b * nq + i)),
                  seq_spec, pl.BlockSpec((nkc, 64, kc), lambda b, i: (b, 0, 0)), seq_spec],
        out_specs=pl.BlockSpec((tq, A_WIDTH), lambda b, i: (b * nq + i, 0)),
        out_shape=jax.ShapeDtypeStruct((T, A_WIDTH), MXU_DTYPE),
        scratch_shapes=[pltpu.VMEM((nkc, kc, tq), I32), pltpu.VMEM((kc, A_HEADS * tq), F32),
                        pltpu.VMEM((kc, A_HEADS * tq), MXU_DTYPE),
                        pltpu.VMEM((1, A_HEADS * tq), F32), pltpu.VMEM((1, A_HEADS * tq), F32),
                        pltpu.VMEM((1, A_HEADS * tq), F32), pltpu.VMEM((A_HEAD_DIM, A_HEADS * tq), F32)],
        compiler_params=_params(("parallel", "arbitrary"), 58),
        name="dsa",
    )(qt, qit, wit, kv, vt, kib)


POOL_HALO = 16


def _pool_kernel(ya_ref, u_ref, halo_ref, pw_ref, ps_ref, o_ref, *, n_pos):
    tm = u_ref.shape[0]
    i = pl.program_id(0)
    seq_start = (i % n_pos) == 0
    u = u_ref[...]
    halo = jnp.where(seq_start, 0.0, halo_ref[...])
    ue = jnp.concatenate([halo, u], axis=0)
    t = (i % n_pos) * tm + lax.broadcasted_iota(I32, (tm, 1), 0)
    o_ref[:, :A_WIDTH] = ya_ref[...]
    for gi, w in enumerate(POOL_WINDOWS):
        lanes = slice(gi * POOL_GROUP, (gi + 1) * POOL_GROUP)
        s = ue[:, lanes]
        span = 1
        while span < w:
            s = s + pltpu.roll(s, span, 0)
            span *= 2
        count = jnp.minimum(t + 1, w).astype(F32)
        pooled = s[POOL_HALO:, :] / count - u[:, lanes]
        y = _dot(pooled.astype(MXU_DTYPE), pw_ref[gi]) * ps_ref[:, lanes]
        o_ref[:, A_WIDTH + gi * POOL_GROUP: A_WIDTH + (gi + 1) * POOL_GROUP] = y.astype(o_ref.dtype)


def _pool(ya, u, pool_w, pool_scale, L, tm):
    T = u.shape[0]
    n_pos = L // tm
    row = lambda i: (i, 0)
    halo_map = lambda i: (jnp.maximum(i * (tm // POOL_HALO) - 1, 0), 0)
    return pl.pallas_call(
        functools.partial(_pool_kernel, n_pos=n_pos),
        grid=(T // tm,),
        in_specs=[pl.BlockSpec((tm, A_WIDTH), row), pl.BlockSpec((tm, POOL_WIDTH), row),
                  pl.BlockSpec((POOL_HALO, POOL_WIDTH), halo_map),
                  _resident(pool_w.shape), _resident((1, POOL_WIDTH))],
        out_specs=pl.BlockSpec((tm, A_WIDTH + POOL_WIDTH), row),
        out_shape=jax.ShapeDtypeStruct((T, A_WIDTH + POOL_WIDTH), MXU_DTYPE),
        compiler_params=_params(("parallel",), 32),
        name="pool",
    )(ya, u, u, pool_w, pool_scale)


def _odd_proj_kernel(h_ref, g_ref, w_ref, qkv_ref, z_ref, ba_ref):
    xn = _rms(h_ref[...], g_ref[...]).astype(MXU_DTYPE)
    n_qkv = qkv_ref.shape[1]
    n_z = z_ref.shape[1]
    step = 512
    for c in range(n_qkv // step):
        qkv_ref[:, c * step:(c + 1) * step] = _dot(xn, w_ref[:, c * step:(c + 1) * step])
    for c in range(n_z // step):
        z_ref[:, c * step:(c + 1) * step] = _dot(xn, w_ref[:, n_qkv + c * step: n_qkv + (c + 1) * step])
    ba_ref[...] = _dot(xn, w_ref[:, n_qkv + n_z:])


def _odd_proj(h2, g, w_all, tm):
    T, D = h2.shape
    N = w_all.shape[1]
    row = lambda i: (i, 0)
    return pl.pallas_call(
        _odd_proj_kernel,
        grid=(T // tm,),
        in_specs=[pl.BlockSpec((tm, D), row), _resident((1, D)), _resident((D, N))],
        out_specs=[pl.BlockSpec((tm, 3 * C_WIDTH), row), pl.BlockSpec((tm, C_WIDTH), row), pl.BlockSpec((tm, LANES), row)],
        out_shape=[jax.ShapeDtypeStruct((T, 3 * C_WIDTH), F32), jax.ShapeDtypeStruct((T, C_WIDTH), F32),
                   jax.ShapeDtypeStruct((T, LANES), F32)],
        compiler_params=_params(("parallel",), 48),
        name="odd_proj",
    )(h2, g, w_all)


def _gates_kernel(ba_ref, alog_ref, dtb_ref, gate_ref, cum_ref):
    tm = ba_ref.shape[0]
    x = ba_ref[...]
    lane = lax.broadcasted_iota(I32, x.shape, 1)
    beta = jax.nn.sigmoid(x)
    g = -jnp.exp(alog_ref[...]) * jnp.logaddexp(x + dtb_ref[...], 0.0)
    gate = jnp.where(lane < C_HEADS, beta, jnp.where(lane < 2 * C_HEADS, g, 0.0))
    gate_ref[...] = gate
    ii = lax.broadcasted_iota(I32, (CHUNK, CHUNK), 0)
    jj = lax.broadcasted_iota(I32, (CHUNK, CHUNK), 1)
    tri = (ii >= jj).astype(F32)
    for c in range(tm // CHUNK):
        rows = slice(c * CHUNK, (c + 1) * CHUNK)
        cum_ref[rows, :] = _dot3(tri, gate[rows, :])


def _gates(ba, alog_row, dtb_row, tm):
    T = ba.shape[0]
    row = lambda i: (i, 0)
    return pl.pallas_call(
        _gates_kernel,
        grid=(T // tm,),
        in_specs=[pl.BlockSpec((tm, LANES), row), _resident((1, LANES)), _resident((1, LANES))],
        out_specs=[pl.BlockSpec((tm, LANES), row), pl.BlockSpec((tm, LANES), row)],
        out_shape=[jax.ShapeDtypeStruct((T, LANES), F32), jax.ShapeDtypeStruct((T, LANES), F32)],
        compiler_params=_params(("parallel",), 16),
        name="gates",
    )(ba, alog_row, dtb_row)


CONV_HALO = 8


def _gdn_prep_kernel(xq_ref, xk_ref, xv_ref, hq_ref, hk_ref, hv_ref, wq_ref, wk_ref, wv_ref, gcol_ref, grow_ref,
                     kcd_ref, qd_ref, kdt_ref, attn_ref, val_ref):
    rows = xq_ref.shape[0]
    seq_start = pl.program_id(2) == 0

    def conv_silu(x_ref, halo_ref, w_ref):
        x = x_ref[...]
        halo = jnp.where(seq_start, 0.0, halo_ref[...])
        xe = jnp.concatenate([halo, x], axis=0)
        w = w_ref[...]
        y = w[CONV_WIDTH - 1:CONV_WIDTH, :] * x
        for j in range(CONV_WIDTH - 1):
            back = CONV_WIDTH - 1 - j
            y = y + w[j:j + 1, :] * pltpu.roll(xe, back, 0)[CONV_HALO:, :]
        return _silu(y)

    def l2n(x):
        return x * lax.rsqrt(jnp.sum(x * x, axis=-1, keepdims=True) + NORM_EPS)

    q = l2n(conv_silu(xq_ref, hq_ref, wq_ref)) * (C_HEAD_DIM ** -0.5)
    k = l2n(conv_silu(xk_ref, hk_ref, wk_ref))
    v = conv_silu(xv_ref, hv_ref, wv_ref)
    gcol = gcol_ref[0]
    grow = grow_ref[0, 0]

    ii = lax.broadcasted_iota(I32, (CHUNK, CHUNK), 0)
    jj = lax.broadcasted_iota(I32, (CHUNK, CHUNK), 1)
    tril = ii >= jj
    strict = ii > jj
    eye = (ii == jj).astype(F32)

    sls = [slice(c * CHUNK, (c + 1) * CHUNK) for c in range(rows // CHUNK)]
    beta = [gcol[s, 0:1] for s in sls]
    gc = [gcol[s, 1:2] for s in sls]
    decay = [jnp.exp(jnp.where(tril, g - grow[1:2, s], -jnp.inf)) for g, s in zip(gc, sls)]
    kb = [k[s].astype(MXU_DTYPE) for s in sls]
    kk = [_dot_nt(x, x) for x in kb]
    lmat = [jnp.where(strict, b * x * d, 0.0) for b, x, d in zip(beta, kk, decay)]
    tmat = [eye - jnp.where((ii >> 1) == (jj >> 1), x, 0.0) for x in lmat]
    shift = 1
    while (2 << shift) <= CHUNK:
        couple = jnp.logical_and((ii >> (shift + 1)) == (jj >> (shift + 1)), (ii >> shift) != (jj >> shift))
        cd = [_dot(jnp.where(couple, x, 0.0).astype(MXU_DTYPE), t.astype(MXU_DTYPE)) for x, t in zip(lmat, tmat)]
        tmat = [t - _dot(t.astype(MXU_DTYPE), y.astype(MXU_DTYPE)) for t, y in zip(tmat, cd)]
        shift += 1
    eg = [jnp.exp(g) for g in gc]
    for c, s in enumerate(sls):
        rhs = jnp.concatenate([v[s] * beta[c], k[s] * (beta[c] * eg[c])], axis=1).astype(MXU_DTYPE)
        solved = _dot(tmat[c].astype(MXU_DTYPE), rhs)
        val_ref[s, :] = solved[:, :C_HEAD_DIM]
        kcd_ref[s, :] = solved[:, C_HEAD_DIM:].astype(kcd_ref.dtype)
        attn_ref[s, :] = (_dot_nt(q[s].astype(MXU_DTYPE), kb[c]) * decay[c]).astype(attn_ref.dtype)
        qd_ref[s, :] = (q[s] * eg[c]).astype(qd_ref.dtype)
        g_last = gc[c][CHUNK - 1:CHUNK, :]
        kdt_ref[s, :] = (k[s] * jnp.exp(g_last - gc[c])).T.astype(kdt_ref.dtype)


def _gdn_prep(qkv, conv_w, gcol, grow, B, L, rows):
    T = B * L
    H = C_HEADS
    nr = L // rows
    hb = rows // CONV_HALO

    def x_spec(part):
        return pl.BlockSpec((rows, C_HEAD_DIM), lambda b, h, r: (b * nr + r, part * H + h))

    def halo_spec(part):
        return pl.BlockSpec((CONV_HALO, C_HEAD_DIM), lambda b, h, r: (jnp.maximum((b * nr + r) * hb - 1, 0), part * H + h))

    def w_spec(part):
        return pl.BlockSpec((CONV_WIDTH, C_HEAD_DIM), lambda b, h, r: (0, part * H + h))

    out_spec = pl.BlockSpec((rows, C_HEAD_DIM), lambda b, h, r: (b * nr + r, h))
    mm_shape = jax.ShapeDtypeStruct((T, C_WIDTH), MXU_DTYPE)
    return pl.pallas_call(
        _gdn_prep_kernel,
        grid=(B, H, nr),
        in_specs=[x_spec(0), x_spec(1), x_spec(2), halo_spec(0), halo_spec(1), halo_spec(2),
                  w_spec(0), w_spec(1), w_spec(2),
                  pl.BlockSpec((1, rows, 2), lambda b, h, r: (h, b * nr + r, 0)),
                  pl.BlockSpec((1, 1, 2, rows), lambda b, h, r: (b, h, 0, r))],
        out_specs=[out_spec] * 5,
        out_shape=[mm_shape, mm_shape, mm_shape, mm_shape, jax.ShapeDtypeStruct((T, C_WIDTH), F32)],
        compiler_params=_params(("parallel", "parallel", "parallel"), 32),
        name="gdn_prep",
    )(qkv, qkv, qkv, qkv, qkv, qkv, conv_w, conv_w, conv_w, gcol, grow)


def _gdn_scan_kernel(kcd_ref, qd_ref, kdt_ref, attn_ref, val_ref, grow_ref, z_ref, on_ref, y_ref, state_ref):
    rows = kcd_ref.shape[0]
    nh = C_HEADS

    @pl.when(pl.program_id(1) == 0)
    def _():
        state_ref[...] = jnp.zeros(state_ref.shape, F32)

    on = on_ref[...]
    for c in range(rows // CHUNK):
        s = slice(c * CHUNK, (c + 1) * CHUNK)
        cols = [slice(h * C_HEAD_DIM, (h + 1) * C_HEAD_DIM) for h in range(nh)]
        state = [state_ref[h] for h in range(nh)]
        both = [_dot(jnp.concatenate([kcd_ref[s, cl], qd_ref[s, cl]], axis=0), st.astype(MXU_DTYPE))
                for cl, st in zip(cols, state)]
        v_new = [(val_ref[s, cl] - b[:CHUNK]).astype(MXU_DTYPE) for cl, b in zip(cols, both)]
        res = [_dot(jnp.concatenate([attn_ref[s, cl], kdt_ref[s, cl]], axis=0), vn) for cl, vn in zip(cols, v_new)]
        for h in range(nh):
            g_last = grow_ref[0, h, 1:2, (c + 1) * CHUNK - 1:(c + 1) * CHUNK]
            state_ref[h] = state[h] * jnp.exp(g_last) + res[h][CHUNK:]
            o = both[h][CHUNK:] + res[h][:CHUNK]
            y = _rms(o, on) * _silu(z_ref[s, cols[h]])
            y_ref[s, cols[h]] = y.astype(y_ref.dtype)


def _gdn_scan(kcd, qd, kdt, attn, val, grow, z, o_norm, B, L, rows):
    T = B * L
    nr = L // rows
    row = pl.BlockSpec((rows, C_WIDTH), lambda b, r: (b * nr + r, 0))
    return pl.pallas_call(
        _gdn_scan_kernel,
        grid=(B, nr),
        in_specs=[row, row, row, row, row,
                  pl.BlockSpec((1, C_HEADS, 2, rows), lambda b, r: (b, 0, 0, r)),
                  row, pl.BlockSpec((1, C_HEAD_DIM), lambda b, r: (0, 0))],
        out_specs=row,
        out_shape=jax.ShapeDtypeStruct((T, C_WIDTH), MXU_DTYPE),
        scratch_shapes=[pltpu.VMEM((C_HEADS, C_HEAD_DIM, C_HEAD_DIM), F32)],
        compiler_params=_params(("parallel", "arbitrary"), 48),
        name="gdn_scan",
    )(kcd, qd, kdt, attn, val, grow, z, o_norm)


def _out_mlp_kernel(h_ref, y_ref, wo_ref, g_ref, w1_ref, w2_ref, fg_ref, o_ref, *, ff_chunk, final):
    h1 = h_ref[...] + _dot(y_ref[...], wo_ref[...])
    xn = _rms(h1, g_ref[...]).astype(MXU_DTYPE)
    acc = h1
    for c in range(w1_ref.shape[1] // ff_chunk):
        cols = slice(c * ff_chunk, (c + 1) * ff_chunk)
        a = jnp.square(jnp.maximum(_dot(xn, w1_ref[:, cols]), 0.0)).astype(MXU_DTYPE)
        acc = acc + _dot(a, w2_ref[cols, :])
    if final:
        acc = _rms(acc, fg_ref[...])
    o_ref[...] = acc


def _out_mlp(h2, y, w_out, g, w1, w2, final_g, final, tm):
    T, D = h2.shape
    dff = w1.shape[1]
    row = lambda i: (i, 0)
    return pl.pallas_call(
        functools.partial(_out_mlp_kernel, ff_chunk=512, final=final),
        grid=(T // tm,),
        in_specs=[pl.BlockSpec((tm, D), row), pl.BlockSpec((tm, y.shape[1]), row), _resident(w_out.shape),
                  _resident((1, D)), _resident((D, dff)), _resident((dff, D)), _resident((1, D))],
        out_specs=pl.BlockSpec((tm, D), row),
        out_shape=jax.ShapeDtypeStruct((T, D), F32),
        compiler_params=_params(("parallel",), 48),
        name="out_mlp",
    )(h2, y, w_out, g, w1, w2, final_g)


def _rope_tables(L):
    inv = ROPE_THETA ** (-jnp.arange(0, A_HEAD_DIM, 2, dtype=F32) / A_HEAD_DIM)
    ang = jnp.arange(L, dtype=F32)[:, None] * inv[None, :]
    cos, sin = jnp.cos(ang), jnp.sin(ang)
    one, zero = jnp.ones((L, 64), F32), jnp.zeros((L, 64), F32)
    cos2 = jnp.concatenate([cos, cos, cos, cos], axis=1)
    sin2 = jnp.concatenate([-sin, sin, -sin, sin], axis=1)
    cosk = jnp.concatenate([cos, cos, one], axis=1)
    sink = jnp.concatenate([-sin, sin, zero], axis=1)
    return cos2, sin2, cosk, sink


def _pick_tile(L, want):
    t = min(want, L)
    while L % t:
        t //= 2
    return t


def kernel(x, mix_norm, mlp_norm, w_ff1, w_ff2, ev_w_in, ev_kv_norm, ev_w_uk, ev_w_uv, ev_pool_w, ev_pool_scale, ev_w_out, od_w_in, od_conv_w, od_a_log, od_dt_bias, od_o_norm, od_w_out, final_norm):
    B, L, D = x.shape
    T = B * L
    depth = mix_norm.shape[0]
    mm = MXU_DTYPE
    h = x.reshape(T, D)
    tabs = _rope_tables(L)
    tm = _pick_tile(L, 512)
    tm_mlp = _pick_tile(L, 512)
    tq = _pick_tile(L, 512)
    kc = _pick_tile(L, 512)
    gdn_rows = _pick_tile(L, 512)
    fg = final_norm.reshape(1, D)

    for layer in range(depth):
        j = layer // 2
        g_mix = mix_norm[layer].reshape(1, D)
        if layer % 2 == 0:
            w = ev_w_in[j]
            c0 = A_WIDTH
            c1 = c0 + A_KV_RANK
            c2 = c1 + IDX_HEADS * IDX_DIM
            c3 = c2 + IDX_DIM
            c4 = c3 + IDX_HEADS
            pad = jnp.zeros((D, LANES - IDX_DIM - IDX_HEADS), w.dtype)
            w_all = jnp.concatenate([w[:, :c0], w[:, c1:c2], w[:, c4:], w[:, c0:c1], w[:, c2:c3], w[:, c3:c4], pad], axis=1).astype(mm)
            wkv = jnp.concatenate([ev_w_uk[j], ev_w_uv[j]], axis=1).astype(mm)
            qt, qit, u, kv, vt, kib, wit = _even_proj(h, g_mix, w_all, ev_kv_norm[j].reshape(1, -1), wkv, tabs, L, tm, tq, kc)
            ya = _dsa(qt, qit, wit, kv, vt, kib, B, L, tq, kc)
            y = _pool(ya, u, ev_pool_w[j].astype(mm), ev_pool_scale[j].reshape(1, -1), L, tm)
            w_out = ev_w_out[j].astype(mm)
        else:
            w = od_w_in[j]
            pad = jnp.zeros((D, LANES - 2 * C_HEADS), w.dtype)
            w_all = jnp.concatenate([w, pad], axis=1).astype(mm)
            qkv, z, ba = _odd_proj(h, g_mix, w_all, tm)
            zero8 = jnp.zeros((C_HEADS,), F32)
            lane_pad = jnp.zeros((LANES - 2 * C_HEADS,), F32)
            alog_row = jnp.concatenate([zero8, od_a_log[j], lane_pad]).reshape(1, LANES)
            dtb_row = jnp.concatenate([zero8, od_dt_bias[j], lane_pad]).reshape(1, LANES)
            gate, cum = _gates(ba, alog_row, dtb_row, tm)
            beta = gate[:, :C_HEADS]
            gcum = cum[:, C_HEADS:2 * C_HEADS]
            gcol = jnp.stack([beta.T, gcum.T], axis=-1)
            grow = jnp.stack([beta.reshape(B, L, C_HEADS), gcum.reshape(B, L, C_HEADS)], axis=1)
            grow = jnp.transpose(grow, (0, 3, 1, 2))
            kcd, qd, kdt, attn, val = _gdn_prep(qkv, od_conv_w[j], gcol, grow, B, L, gdn_rows)
            y = _gdn_scan(kcd, qd, kdt, attn, val, grow, z, od_o_norm[j].reshape(1, -1), B, L, gdn_rows)
            w_out = od_w_out[j].astype(mm)
        h = _out_mlp(h, y, w_out, mlp_norm[layer].reshape(1, D), w_ff1[layer].astype(mm), w_ff2[layer].astype(mm),
                     fg, layer == depth - 1, tm_mlp)
    return h.reshape(B, L, D)
```

```python
import functools

import numpy as np
import jax
import jax.numpy as jnp
from jax import lax
from jax.experimental import pallas as pl
from jax.experimental.pallas import tpu as pltpu

F32 = jnp.float32
I32 = jnp.int32
MXU_DTYPE = jnp.bfloat16

A_HEADS = 8
A_HEAD_DIM = 64
A_WIDTH = A_HEADS * A_HEAD_DIM
A_KV_RANK = 128
IDX_HEADS = 8
IDX_DIM = 64
TOPK_MAX = 256
POOL_WINDOWS = (2, 4, 8, 16)
POOL_GROUP = 128
POOL_WIDTH = POOL_GROUP * len(POOL_WINDOWS)
C_HEADS = 8
C_HEAD_DIM = 128
C_WIDTH = C_HEADS * C_HEAD_DIM
CONV_WIDTH = 4
CHUNK = 128
ROPE_THETA = 10000.0
NORM_EPS = 1e-6

LANES = 128
INT_MIN = np.int32(-2 ** 31)
ABOVE = np.int32(2 ** 31 - 1)
BELOW = np.int32(2 ** 31 - 2)
MASKED_LOGIT = -1e30
COUNT_ROWS = 32
LOG2_E = float(np.log2(np.e))

NT_DIMS = (((1,), (1,)), ((), ()))


def _params(semantics, vmem_mb):
    return pltpu.CompilerParams(dimension_semantics=semantics, vmem_limit_bytes=vmem_mb * 1024 * 1024)


def _dot(a, b):
    return jnp.dot(a, b, preferred_element_type=F32)


def _dot_nt(a, b):
    return lax.dot_general(a, b, NT_DIMS, preferred_element_type=F32)


def _split(a):
    hi = a.astype(MXU_DTYPE)
    lo = (a - hi.astype(F32)).astype(MXU_DTYPE)
    return hi, lo


def _dot3(a, b):
    ah, al = _split(a)
    bh, bl = _split(b)
    return _dot(ah, bh) + (_dot(ah, bl) + _dot(al, bh))


def _rms(x, g):
    return x * lax.rsqrt(jnp.mean(x * x, axis=-1, keepdims=True) + NORM_EPS) * g


def _silu(x):
    return x * jax.nn.sigmoid(x)


def _resident(shape):
    nd = len(shape)
    return pl.BlockSpec(shape, lambda *_: (0,) * nd, pipeline_mode=pl.Buffered(1))


def _rope(x, cos, sin_signed, first_half):
    partner = jnp.where(first_half, pltpu.roll(x, LANES - 32, 1), pltpu.roll(x, 32, 1))
    return x * cos + partner * sin_signed


def _even_proj_kernel(h_ref, g_ref, w_ref, kvn_ref, wkv_ref, cos_ref, sin_ref, cosk_ref, sink_ref,
                      qt_ref, qit_ref, u_ref, kv_ref, vt_ref, kib_ref, wit_ref, *, tq, kc):
    tm = h_ref.shape[0]
    xn = _rms(h_ref[...], g_ref[...]).astype(MXU_DTYPE)
    p = _dot(xn, w_ref[...])
    lane = lax.broadcasted_iota(I32, (tm, LANES), 1)
    first_half = (lane % 64) < 32
    cos, sin = cos_ref[...], sin_ref[...]
    cosk, sink = cosk_ref[...], sink_ref[...]

    def heads_out(o_ref, base, scale):
        for s in range(4):
            r = _rope(p[:, base + s * LANES: base + (s + 1) * LANES], cos, sin, first_half)
            if scale != 1.0:
                r = r * scale
            for j in range(tm // tq):
                rt = r[j * tq:(j + 1) * tq, :].T.astype(o_ref.dtype)
                o_ref[j, :, 2 * s * tq:(2 * s + 1) * tq] = rt[:64, :]
                o_ref[j, :, (2 * s + 1) * tq:(2 * s + 2) * tq] = rt[64:, :]

    heads_out(qt_ref, 0, (A_HEAD_DIM ** -0.5) * LOG2_E)
    heads_out(qit_ref, A_WIDTH, 1.0)
    u_ref[...] = p[:, 1024:1536]
    ckv = _rms(p[:, 1536:1664], kvn_ref[...]).astype(MXU_DTYPE)
    kv = _rope(_dot(ckv, wkv_ref[...]), cosk, sink, first_half)
    kv_ref[...] = kv.astype(kv_ref.dtype)
    for j in range(tm // kc):
        vt_ref[j] = kv[j * kc:(j + 1) * kc, :].T[64:, :].astype(vt_ref.dtype)
    kiw = _rope(p[:, 1664:1792], cosk, sink, first_half)
    kib_ref[...] = kiw.astype(kib_ref.dtype)
    wit = kiw.T[64:64 + IDX_HEADS, :]
    wit_ref[...] = (wit * (IDX_HEADS ** -0.5)) * (IDX_DIM ** -0.5)


def _even_proj(h2, g, w_all, kvn, wkv, tabs, L, tm, tq, kc):
    T, D = h2.shape
    n_pos = L // tm
    N = w_all.shape[1]
    row = lambda i: (i, 0)
    pos = lambda i: (i % n_pos, 0)
    tab_spec = pl.BlockSpec((tm, LANES), pos)
    head_spec = pl.BlockSpec((tm // tq, 64, A_HEADS * tq), lambda i: (i, 0, 0))
    head_shape = jax.ShapeDtypeStruct((T // tq, 64, A_HEADS * tq), MXU_DTYPE)
    return pl.pallas_call(
        functools.partial(_even_proj_kernel, tq=tq, kc=kc),
        grid=(T // tm,),
        in_specs=[pl.BlockSpec((tm, D), row), _resident((1, D)), _resident((D, N)), _resident((1, A_KV_RANK)),
                  _resident((A_KV_RANK, LANES)), tab_spec, tab_spec, tab_spec, tab_spec],
        out_specs=[head_spec, head_spec, pl.BlockSpec((tm, POOL_WIDTH), row), pl.BlockSpec((tm, LANES), row),
                   pl.BlockSpec((tm // kc, 64, kc), lambda i: (i, 0, 0)), pl.BlockSpec((tm, LANES), row),
                   pl.BlockSpec((IDX_HEADS, tm), lambda i: (0, i))],
        out_shape=[head_shape, head_shape, jax.ShapeDtypeStruct((T, POOL_WIDTH), F32),
                   jax.ShapeDtypeStruct((T, LANES), MXU_DTYPE), jax.ShapeDtypeStruct((T // kc, 64, kc), MXU_DTYPE),
                   jax.ShapeDtypeStruct((T, LANES), MXU_DTYPE), jax.ShapeDtypeStruct((IDX_HEADS, T), F32)],
        compiler_params=_params(("parallel",), 40),
        name="even_proj",
    )(h2, g, w_all, kvn, wkv, *tabs)


def _dsa_kernel(qt_ref, qit_ref, wit_ref, kv_ref, vt_ref, kib_ref, o_ref,
                keys_ref, s_ref, p_ref, m_ref, l_ref, a_ref, acc_ref, *, tq, kc, topk, pos_bits):
    i = pl.program_id(1)
    n_chunks = ((i + 1) * tq + kc - 1) // kc
    nh = A_HEADS
    qpos = i * tq + lax.broadcasted_iota(I32, (1, tq), 1)
    kpos0 = lax.broadcasted_iota(I32, (kc, tq), 0)
    wit = wit_ref[...]

    def score_chunk(c, carry):
        kic = kib_ref[pl.ds(pl.multiple_of(c * kc, kc), kc), :]
        s_ref[...] = _dot(kic[:, :IDX_DIM], qit_ref[0])
        score = jnp.zeros((kc, tq), F32)
        for h in range(nh):
            score = score + jnp.maximum(s_ref[:, h * tq:(h + 1) * tq], 0.0) * wit[h:h + 1, :]
        bits = lax.bitcast_convert_type(score, I32)
        mag = bits & np.int32(0x7FFFFFFF)
        key = jnp.where(bits < 0, -mag, mag)
        keys_ref[c] = jnp.where(c * kc + kpos0 <= qpos, key, INT_MIN)
        return carry

    lax.fori_loop(0, n_chunks, score_chunk, 0)

    def count(pred):
        def body(c, acc):
            w = jnp.where(pred(keys_ref[c], c), 1, 0)
            return acc + jnp.sum(w.reshape(kc // COUNT_ROWS, COUNT_ROWS, tq), axis=0)
        acc = lax.fori_loop(0, n_chunks, body, jnp.zeros((COUNT_ROWS, tq), I32))
        return jnp.sum(acc, axis=0, keepdims=True)

    k_eff = jnp.minimum(qpos + 1, topk)

    def thr_bit(it, thr):
        cand = thr ^ lax.shift_left(np.int32(1), 31 - it)
        cnt = count(lambda kb, c: kb >= cand)
        return jnp.where(cnt >= k_eff, cand, thr)

    thr = lax.fori_loop(0, 32, thr_bit, jnp.full((1, tq), INT_MIN, I32))

    def recode(c, carry):
        n_gt, n_eq = carry
        kb = keys_ref[c]
        gt = kb > thr
        eq = kb == thr
        keys_ref[c] = jnp.where(gt, ABOVE, jnp.where(eq, c * kc + kpos0, BELOW))
        fold = lambda m: jnp.sum(jnp.where(m, 1, 0).reshape(kc // COUNT_ROWS, COUNT_ROWS, tq), axis=0)
        return n_gt + fold(gt), n_eq + fold(eq)

    zero_counts = jnp.zeros((COUNT_ROWS, tq), I32)
    n_gt, n_eq = lax.fori_loop(0, n_chunks, recode, (zero_counts, zero_counts))
    n_gt = jnp.sum(n_gt, axis=0, keepdims=True)
    n_eq = jnp.sum(n_eq, axis=0, keepdims=True)

    need = k_eff - n_gt
    has_tie = jnp.max(n_gt + n_eq - k_eff) > 0

    def cut_bit(it, cut):
        cand = cut | lax.shift_left(np.int32(1), pos_bits - 1 - it)
        cnt = count(lambda kb, c: kb < cand)
        return jnp.where(cnt < need, cand, cut)

    cut = lax.cond(has_tie,
                   lambda: lax.fori_loop(0, pos_bits, cut_bit, jnp.zeros((1, tq), I32)),
                   lambda: jnp.full((1, tq), 2 ** pos_bits - 1, I32))

    m_ref[...] = jnp.full(m_ref.shape, MASKED_LOGIT, F32)
    l_ref[...] = jnp.zeros(l_ref.shape, F32)
    acc_ref[...] = jnp.zeros(acc_ref.shape, F32)

    def attend_chunk(c, carry):
        kvc = kv_ref[pl.ds(pl.multiple_of(c * kc, kc), kc), :][:, :A_HEAD_DIM]
        kb = keys_ref[c]
        sel = jnp.logical_or(kb == ABOVE, kb <= cut)
        cap = jnp.where(sel, jnp.inf, MASKED_LOGIT)
        for h in range(nh):
            cols = slice(h * tq, (h + 1) * tq)
            s = jnp.minimum(_dot(kvc, qt_ref[0, :, cols]), cap)
            s_ref[:, cols] = s
            m_old = m_ref[:, cols]
            m_new = jnp.maximum(m_old, jnp.max(s, axis=0, keepdims=True))
            m_ref[:, cols] = m_new
            a_ref[:, cols] = jnp.exp2(m_old - m_new)
        for h in range(nh):
            cols = slice(h * tq, (h + 1) * tq)
            p = jnp.exp2(s_ref[:, cols] - m_ref[:, cols])
            l_ref[:, cols] = a_ref[:, cols] * l_ref[:, cols] + jnp.sum(p, axis=0, keepdims=True)
            p_ref[:, cols] = p.astype(p_ref.dtype)
        acc_ref[...] = acc_ref[...] * a_ref[...] + _dot(vt_ref[c], p_ref[...])
        return carry

    lax.fori_loop(0, n_chunks, attend_chunk, 0)

    out_t = jnp.concatenate([acc_ref[:, h * tq:(h + 1) * tq] / l_ref[:, h * tq:(h + 1) * tq] for h in range(nh)], axis=0)
    o_ref[...] = out_t.T.astype(o_ref.dtype)


def _dsa(qt, qit, wit, kv, vt, kib, B, L, tq, kc):
    T = B * L
    nq = L // tq
    nkc = L // kc
    topk = min(TOPK_MAX, L // 4)
    pos_bits = max(1, int(np.ceil(np.log2(L))))
    head_spec = pl.BlockSpec((1, 64, A_HEADS * tq), lambda b, i: (b * nq + i, 0, 0))
    seq_spec = pl.BlockSpec((L, LANES), lambda b, i: (b, 0))
    kern = functools.partial(_dsa_kernel, tq=tq, kc=kc, topk=topk, pos_bits=pos_bits)
    return pl.pallas_call(
        kern,
        grid=(B, nq),
        in_specs=[head_spec, head_spec, pl.BlockSpec((IDX_HEADS, tq), lambda b, i: (0, b * nq + i)),
                  seq_spec, pl.BlockSpec((nkc, 64, kc), lambda b, i: (b, 0, 0)), seq_spec],
        out_specs=pl.BlockSpec((tq, A_WIDTH), lambda b, i: (b * nq + i, 0)),
        out_shape=jax.ShapeDtypeStruct((T, A_WIDTH), MXU_DTYPE),
        scratch_shapes=[pltpu.VMEM((nkc, kc, tq), I32), pltpu.VMEM((kc, A_HEADS * tq), F32),
                        pltpu.VMEM((kc, A_HEADS * tq), MXU_DTYPE),
                        pltpu.VMEM((1, A_HEADS * tq), F32), pltpu.VMEM((1, A_HEADS * tq), F32),
                        pltpu.VMEM((1, A_HEADS * tq), F32), pltpu.VMEM((A_HEAD_DIM, A_HEADS * tq), F32)],
        compiler_params=_params(("parallel", "arbitrary"), 58),
        name="dsa",
    )(qt, qit, wit, kv, vt, kib)


POOL_HALO = 16


def _pool_kernel(ya_ref, u_ref, halo_ref, pw_ref, ps_ref, o_ref, *, n_pos):
    tm = u_ref.shape[0]
    i = pl.program_id(0)
    seq_start = (i % n_pos) == 0
    u = u_ref[...]
    halo = jnp.where(seq_start, 0.0, halo_ref[...])
    ue = jnp.concatenate([halo, u], axis=0)
    t = (i % n_pos) * tm + lax.broadcasted_iota(I32, (tm, 1), 0)
    o_ref[:, :A_WIDTH] = ya_ref[...]
    for gi, w in enumerate(POOL_WINDOWS):
        lanes = slice(gi * POOL_GROUP, (gi + 1) * POOL_GROUP)
        s = ue[:, lanes]
        span = 1
        while span < w:
            s = s + pltpu.roll(s, span, 0)
            span *= 2
        count = jnp.minimum(t + 1, w).astype(F32)
        pooled = s[POOL_HALO:, :] / count - u[:, lanes]
        y = _dot(pooled.astype(MXU_DTYPE), pw_ref[gi]) * ps_ref[:, lanes]
        o_ref[:, A_WIDTH + gi * POOL_GROUP: A_WIDTH + (gi + 1) * POOL_GROUP] = y.astype(o_ref.dtype)


def _pool(ya, u, pool_w, pool_scale, L, tm):
    T = u.shape[0]
    n_pos = L // tm
    row = lambda i: (i, 0)
    halo_map = lambda i: (jnp.maximum(i * (tm // POOL_HALO) - 1, 0), 0)
    return pl.pallas_call(
        functools.partial(_pool_kernel, n_pos=n_pos),
        grid=(T // tm,),
        in_specs=[pl.BlockSpec((tm, A_WIDTH), row), pl.BlockSpec((tm, POOL_WIDTH), row),
                  pl.BlockSpec((POOL_HALO, POOL_WIDTH), halo_map),
                  _resident(pool_w.shape), _resident((1, POOL_WIDTH))],
        out_specs=pl.BlockSpec((tm, A_WIDTH + POOL_WIDTH), row),
        out_shape=jax.ShapeDtypeStruct((T, A_WIDTH + POOL_WIDTH), MXU_DTYPE),
        compiler_params=_params(("parallel",), 32),
        name="pool",
    )(ya, u, u, pool_w, pool_scale)


def _odd_proj_kernel(h_ref, g_ref, w_ref, qkv_ref, z_ref, ba_ref):
    xn = _rms(h_ref[...], g_ref[...]).astype(MXU_DTYPE)
    n_qkv = qkv_ref.shape[1]
    n_z = z_ref.shape[1]
    step = 512
    for c in range(n_qkv // step):
        qkv_ref[:, c * step:(c + 1) * step] = _dot(xn, w_ref[:, c * step:(c + 1) * step])
    for c in range(n_z // step):
        z_ref[:, c * step:(c + 1) * step] = _dot(xn, w_ref[:, n_qkv + c * step: n_qkv + (c + 1) * step])
    ba_ref[...] = _dot(xn, w_ref[:, n_qkv + n_z:])


def _odd_proj(h2, g, w_all, tm):
    T, D = h2.shape
    N = w_all.shape[1]
    row = lambda i: (i, 0)
    return pl.pallas_call(
        _odd_proj_kernel,
        grid=(T // tm,),
        in_specs=[pl.BlockSpec((tm, D), row), _resident((1, D)), _resident((D, N))],
        out_specs=[pl.BlockSpec((tm, 3 * C_WIDTH), row), pl.BlockSpec((tm, C_WIDTH), row), pl.BlockSpec((tm, LANES), row)],
        out_shape=[jax.ShapeDtypeStruct((T, 3 * C_WIDTH), F32), jax.ShapeDtypeStruct((T, C_WIDTH), F32),
                   jax.ShapeDtypeStruct((T, LANES), F32)],
        compiler_params=_params(("parallel",), 48),
        name="odd_proj",
    )(h2, g, w_all)


def _gates_kernel(ba_ref, alog_ref, dtb_ref, gate_ref, cum_ref):
    tm = ba_ref.shape[0]
    x = ba_ref[...]
    lane = lax.broadcasted_iota(I32, x.shape, 1)
    beta = jax.nn.sigmoid(x)
    g = -jnp.exp(alog_ref[...]) * jnp.logaddexp(x + dtb_ref[...], 0.0)
    gate = jnp.where(lane < C_HEADS, beta, jnp.where(lane < 2 * C_HEADS, g, 0.0))
    gate_ref[...] = gate
    ii = lax.broadcasted_iota(I32, (CHUNK, CHUNK), 0)
    jj = lax.broadcasted_iota(I32, (CHUNK, CHUNK), 1)
    tri = (ii >= jj).astype(F32)
    for c in range(tm // CHUNK):
        rows = slice(c * CHUNK, (c + 1) * CHUNK)
        cum_ref[rows, :] = _dot3(tri, gate[rows, :])


def _gates(ba, alog_row, dtb_row, tm):
    T = ba.shape[0]
    row = lambda i: (i, 0)
    return pl.pallas_call(
        _gates_kernel,
        grid=(T // tm,),
        in_specs=[pl.BlockSpec((tm, LANES), row), _resident((1, LANES)), _resident((1, LANES))],
        out_specs=[pl.BlockSpec((tm, LANES), row), pl.BlockSpec((tm, LANES), row)],
        out_shape=[jax.ShapeDtypeStruct((T, LANES), F32), jax.ShapeDtypeStruct((T, LANES), F32)],
        compiler_params=_params(("parallel",), 16),
        name="gates",
    )(ba, alog_row, dtb_row)


CONV_HALO = 8


def _gdn_prep_kernel(xq_ref, xk_ref, xv_ref, hq_ref, hk_ref, hv_ref, wq_ref, wk_ref, wv_ref, gcol_ref, grow_ref,
                     kcd_ref, qd_ref, kdt_ref, attn_ref, val_ref):
    rows = xq_ref.shape[0]
    seq_start = pl.program_id(2) == 0

    def conv_silu(x_ref, halo_ref, w_ref):
        x = x_ref[...]
        halo = jnp.where(seq_start, 0.0, halo_ref[...])
        xe = jnp.concatenate([halo, x], axis=0)
        w = w_ref[...]
        y = w[CONV_WIDTH - 1:CONV_WIDTH, :] * x
        for j in range(CONV_WIDTH - 1):
            back = CONV_WIDTH - 1 - j
            y = y + w[j:j + 1, :] * pltpu.roll(xe, back, 0)[CONV_HALO:, :]
        return _silu(y)

    def l2n(x):
        return x * lax.rsqrt(jnp.sum(x * x, axis=-1, keepdims=True) + NORM_EPS)

    q = l2n(conv_silu(xq_ref, hq_ref, wq_ref)) * (C_HEAD_DIM ** -0.5)
    k = l2n(conv_silu(xk_ref, hk_ref, wk_ref))
    v = conv_silu(xv_ref, hv_ref, wv_ref)
    gcol = gcol_ref[0]
    grow = grow_ref[0, 0]

    ii = lax.broadcasted_iota(I32, (CHUNK, CHUNK), 0)
    jj = lax.broadcasted_iota(I32, (CHUNK, CHUNK), 1)
    tril = ii >= jj
    strict = ii > jj
    eye = (ii == jj).astype(F32)

    sls = [slice(c * CHUNK, (c + 1) * CHUNK) for c in range(rows // CHUNK)]
    beta = [gcol[s, 0:1] for s in sls]
    gc = [gcol[s, 1:2] for s in sls]
    decay = [jnp.exp(jnp.where(tril, g - grow[1:2, s], -jnp.inf)) for g, s in zip(gc, sls)]
    kb = [k[s].astype(MXU_DTYPE) for s in sls]
    kk = [_dot_nt(x, x) for x in kb]
    lmat = [jnp.where(strict, b * x * d, 0.0) for b, x, d in zip(beta, kk, decay)]
    tmat = [eye - jnp.where((ii >> 1) == (jj >> 1), x, 0.0) for x in lmat]
    shift = 1
    while (2 << shift) <= CHUNK:
        couple = jnp.logical_and((ii >> (shift + 1)) == (jj >> (shift + 1)), (ii >> shift) != (jj >> shift))
        cd = [_dot(jnp.where(couple, x, 0.0).astype(MXU_DTYPE), t.astype(MXU_DTYPE)) for x, t in zip(lmat, tmat)]
        tmat = [t - _dot(t.astype(MXU_DTYPE), y.astype(MXU_DTYPE)) for t, y in zip(tmat, cd)]
        shift += 1
    eg = [jnp.exp(g) for g in gc]
    for c, s in enumerate(sls):
        rhs = jnp.concatenate([v[s] * beta[c], k[s] * (beta[c] * eg[c])], axis=1).astype(MXU_DTYPE)
        solved = _dot(tmat[c].astype(MXU_DTYPE), rhs)
        val_ref[s, :] = solved[:, :C_HEAD_DIM]
        kcd_ref[s, :] = solved[:, C_HEAD_DIM:].astype(kcd_ref.dtype)
        attn_ref[s, :] = (_dot_nt(q[s].astype(MXU_DTYPE), kb[c]) * decay[c]).astype(attn_ref.dtype)
        qd_ref[s, :] = (q[s] * eg[c]).astype(qd_ref.dtype)
        g_last = gc[c][CHUNK - 1:CHUNK, :]
        kdt_ref[s, :] = (k[s] * jnp.exp(g_last - gc[c])).T.astype(kdt_ref.dtype)


def _gdn_prep(qkv, conv_w, gcol, grow, B, L, rows):
    T = B * L
    H = C_HEADS
    nr = L // rows
    hb = rows // CONV_HALO

    def x_spec(part):
        return pl.BlockSpec((rows, C_HEAD_DIM), lambda b, h, r: (b * nr + r, part * H + h))

    def halo_spec(part):
        return pl.BlockSpec((CONV_HALO, C_HEAD_DIM), lambda b, h, r: (jnp.maximum((b * nr + r) * hb - 1, 0), part * H + h))

    def w_spec(part):
        return pl.BlockSpec((CONV_WIDTH, C_HEAD_DIM), lambda b, h, r: (0, part * H + h))

    out_spec = pl.BlockSpec((rows, C_HEAD_DIM), lambda b, h, r: (b * nr + r, h))
    mm_shape = jax.ShapeDtypeStruct((T, C_WIDTH), MXU_DTYPE)
    return pl.pallas_call(
        _gdn_prep_kernel,
        grid=(B, H, nr),
        in_specs=[x_spec(0), x_spec(1), x_spec(2), halo_spec(0), halo_spec(1), halo_spec(2),
                  w_spec(0), w_spec(1), w_spec(2),
                  pl.BlockSpec((1, rows, 2), lambda b, h, r: (h, b * nr + r, 0)),
                  pl.BlockSpec((1, 1, 2, rows), lambda b, h, r: (b, h, 0, r))],
        out_specs=[out_spec] * 5,
        out_shape=[mm_shape, mm_shape, mm_shape, mm_shape, jax.ShapeDtypeStruct((T, C_WIDTH), F32)],
        compiler_params=_params(("parallel", "parallel", "parallel"), 32),
        name="gdn_prep",
    )(qkv, qkv, qkv, qkv, qkv, qkv, conv_w, conv_w, conv_w, gcol, grow)


def _gdn_scan_kernel(kcd_ref, qd_ref, kdt_ref, attn_ref, val_ref, grow_ref, z_ref, on_ref, y_ref, state_ref):
    rows = kcd_ref.shape[0]
    nh = C_HEADS

    @pl.when(pl.program_id(1) == 0)
    def _():
        state_ref[...] = jnp.zeros(state_ref.shape, F32)

    on = on_ref[...]
    for c in range(rows // CHUNK):
        s = slice(c * CHUNK, (c + 1) * CHUNK)
        cols = [slice(h * C_HEAD_DIM, (h + 1) * C_HEAD_DIM) for h in range(nh)]
        state = [state_ref[h] for h in range(nh)]
        both = [_dot(jnp.concatenate([kcd_ref[s, cl], qd_ref[s, cl]], axis=0), st.astype(MXU_DTYPE))
                for cl, st in zip(cols, state)]
        v_new = [(val_ref[s, cl] - b[:CHUNK]).astype(MXU_DTYPE) for cl, b in zip(cols, both)]
        res = [_dot(jnp.concatenate([attn_ref[s, cl], kdt_ref[s, cl]], axis=0), vn) for cl, vn in zip(cols, v_new)]
        for h in range(nh):
            g_last = grow_ref[0, h, 1:2, (c + 1) * CHUNK - 1:(c + 1) * CHUNK]
            state_ref[h] = state[h] * jnp.exp(g_last) + res[h][CHUNK:]
            o = both[h][CHUNK:] + res[h][:CHUNK]
            y = _rms(o, on) * _silu(z_ref[s, cols[h]])
            y_ref[s, cols[h]] = y.astype(y_ref.dtype)


def _gdn_scan(kcd, qd, kdt, attn, val, grow, z, o_norm, B, L, rows):
    T = B * L
    nr = L // rows
    row = pl.BlockSpec((rows, C_WIDTH), lambda b, r: (b * nr + r, 0))
    return pl.pallas_call(
        _gdn_scan_kernel,
        grid=(B, nr),
        in_specs=[row, row, row, row, row,
                  pl.BlockSpec((1, C_HEADS, 2, rows), lambda b, r: (b, 0, 0, r)),
                  row, pl.BlockSpec((1, C_HEAD_DIM), lambda b, r: (0, 0))],
        out_specs=row,
        out_shape=jax.ShapeDtypeStruct((T, C_WIDTH), MXU_DTYPE),
        scratch_shapes=[pltpu.VMEM((C_HEADS, C_HEAD_DIM, C_HEAD_DIM), F32)],
        compiler_params=_params(("parallel", "arbitrary"), 48),
        name="gdn_scan",
    )(kcd, qd, kdt, attn, val, grow, z, o_norm)


def _out_mlp_kernel(h_ref, y_ref, wo_ref, g_ref, w1_ref, w2_ref, fg_ref, o_ref, *, ff_chunk, final):
    h1 = h_ref[...] + _dot(y_ref[...], wo_ref[...])
    xn = _rms(h1, g_ref[...]).astype(MXU_DTYPE)
    acc = h1
    for c in range(w1_ref.shape[1] // ff_chunk):
        cols = slice(c * ff_chunk, (c + 1) * ff_chunk)
        a = jnp.square(jnp.maximum(_dot(xn, w1_ref[:, cols]), 0.0)).astype(MXU_DTYPE)
        acc = acc + _dot(a, w2_ref[cols, :])
    if final:
        acc = _rms(acc, fg_ref[...])
    o_ref[...] = acc


def _out_mlp(h2, y, w_out, g, w1, w2, final_g, final, tm):
    T, D = h2.shape
    dff = w1.shape[1]
    row = lambda i: (i, 0)
    return pl.pallas_call(
        functools.partial(_out_mlp_kernel, ff_chunk=512, final=final),
        grid=(T // tm,),
        in_specs=[pl.BlockSpec((tm, D), row), pl.BlockSpec((tm, y.shape[1]), row), _resident(w_out.shape),
                  _resident((1, D)), _resident((D, dff)), _resident((dff, D)), _resident((1, D))],
        out_specs=pl.BlockSpec((tm, D), row),
        out_shape=jax.ShapeDtypeStruct((T, D), F32),
        compiler_params=_params(("parallel",), 48),
        name="out_mlp",
    )(h2, y, w_out, g, w1, w2, final_g)


def _rope_tables(L):
    inv = ROPE_THETA ** (-jnp.arange(0, A_HEAD_DIM, 2, dtype=F32) / A_HEAD_DIM)
    ang = jnp.arange(L, dtype=F32)[:, None] * inv[None, :]
    cos, sin = jnp.cos(ang), jnp.sin(ang)
    one, zero = jnp.ones((L, 64), F32), jnp.zeros((L, 64), F32)
    cos2 = jnp.concatenate([cos, cos, cos, cos], axis=1)
    sin2 = jnp.concatenate([-sin, sin, -sin, sin], axis=1)
    cosk = jnp.concatenate([cos, cos, one], axis=1)
    sink = jnp.concatenate([-sin, sin, zero], axis=1)
    return cos2, sin2, cosk, sink


def _pick_tile(L, want):
    t = min(want, L)
    while L % t:
        t //= 2
    return t


def kernel(x, mix_norm, mlp_norm, w_ff1, w_ff2, ev_w_in, ev_kv_norm, ev_w_uk, ev_w_uv, ev_pool_w, ev_pool_scale, ev_w_out, od_w_in, od_conv_w, od_a_log, od_dt_bias, od_o_norm, od_w_out, final_norm):
    B, L, D = x.shape
    T = B * L
    depth = mix_norm.shape[0]
    mm = MXU_DTYPE
    h = x.reshape(T, D)
    tabs = _rope_tables(L)
    tm = _pick_tile(L, 512)
    tm_mlp = _pick_tile(L, 512)
    tq = _pick_tile(L, 512)
    kc = _pick_tile(L, 512)
    gdn_rows = _pick_tile(L, 1024)
    fg = final_norm.reshape(1, D)

    for layer in range(depth):
        j = layer // 2
        g_mix = mix_norm[layer].reshape(1, D)
        if layer % 2 == 0:
            w = ev_w_in[j]
            c0 = A_WIDTH
            c1 = c0 + A_KV_RANK
            c2 = c1 + IDX_HEADS * IDX_DIM
            c3 = c2 + IDX_DIM
            c4 = c3 + IDX_HEADS
            pad = jnp.zeros((D, LANES - IDX_DIM - IDX_HEADS), w.dtype)
            w_all = jnp.concatenate([w[:, :c0], w[:, c1:c2], w[:, c4:], w[:, c0:c1], w[:, c2:c3], w[:, c3:c4], pad], axis=1).astype(mm)
            wkv = jnp.concatenate([ev_w_uk[j], ev_w_uv[j]], axis=1).astype(mm)
            qt, qit, u, kv, vt, kib, wit = _even_proj(h, g_mix, w_all, ev_kv_norm[j].reshape(1, -1), wkv, tabs, L, tm, tq, kc)
            ya = _dsa(qt, qit, wit, kv, vt, kib, B, L, tq, kc)
            y = _pool(ya, u, ev_pool_w[j].astype(mm), ev_pool_scale[j].reshape(1, -1), L, tm)
            w_out = ev_w_out[j].astype(mm)
        else:
            w = od_w_in[j]
            pad = jnp.zeros((D, LANES - 2 * C_HEADS), w.dtype)
            w_all = jnp.concatenate([w, pad], axis=1).astype(mm)
            qkv, z, ba = _odd_proj(h, g_mix, w_all, tm)
            zero8 = jnp.zeros((C_HEADS,), F32)
            lane_pad = jnp.zeros((LANES - 2 * C_HEADS,), F32)
            alog_row = jnp.concatenate([zero8, od_a_log[j], lane_pad]).reshape(1, LANES)
            dtb_row = jnp.concatenate([zero8, od_dt_bias[j], lane_pad]).reshape(1, LANES)
            gate, cum = _gates(ba, alog_row, dtb_row, tm)
            beta = gate[:, :C_HEADS]
            gcum = cum[:, C_HEADS:2 * C_HEADS]
            gcol = jnp.stack([beta.T, gcum.T], axis=-1)
            grow = jnp.stack([beta.reshape(B, L, C_HEADS), gcum.reshape(B, L, C_HEADS)], axis=1)
            grow = jnp.transpose(grow, (0, 3, 1, 2))
            kcd, qd, kdt, attn, val = _gdn_prep(qkv, od_conv_w[j], gcol, grow, B, L, gdn_rows)
            y = _gdn_scan(kcd, qd, kdt, attn, val, grow, z, od_o_norm[j].reshape(1, -1), B, L, gdn_rows)
            w_out = od_w_out[j].astype(mm)
        h = _out_mlp(h, y, w_out, mlp_norm[layer].reshape(1, D), w_ff1[layer].astype(mm), w_ff2[layer].astype(mm),
                     fg, layer == depth - 1, tm_mlp)
    return h.reshape(B, L, D)
```
